```python
import math
import jax
import jax.numpy as jnp
from jax import lax
import numpy as np

D_MODEL = 2048
BATCH = 1
SEQ = 8192
DEPTH = 4

CHUNK = 64
Q_BLOCK = 128
N_HEADS = 8
HEAD_DIM = 64
MIX_W = N_HEADS * HEAD_DIM
N_BRANCH = 4
D_FF = 256 * ((8 * D_MODEL // 3 + 255) // 256)
RWKV_W_LORA = 64
RWKV_A_LORA = 64
RWKV_G_LORA = 128
CONV_W = 4
ROPE_BASE = 10000.0
NORM_EPS = 1e-6
RWKV_GN_EPS = 64e-5
FGATE_BIAS_LO = 3.0
FGATE_BIAS_HI = 6.0

FOX_SIZES = (MIX_W, MIX_W, MIX_W, MIX_W, N_HEADS)
RWKV_SIZES = (MIX_W, MIX_W, MIX_W, RWKV_W_LORA, RWKV_A_LORA, RWKV_G_LORA)
RET_SIZES = (MIX_W, MIX_W, MIX_W, MIX_W)
MLSTM_SIZES = (2 * MIX_W, MIX_W, MIX_W, N_HEADS, N_HEADS)
FOX_W = sum(FOX_SIZES)
RWKV_W = sum(RWKV_SIZES)
RET_W = sum(RET_SIZES)
MLSTM_W = sum(MLSTM_SIZES)
D_IN = FOX_W + RWKV_W + RET_W + MLSTM_W

kernel_name = 'hybrid_chunk_causal_encoder_trunk'


def _split(z, sizes):
    outs, off = [], 0
    for s in sizes:
        outs.append(z[..., off:off + s])
        off += s
    return outs


def _rms(t, eps=NORM_EPS):
    tf = t.astype(jnp.float32)
    return tf * lax.rsqrt(jnp.mean(tf * tf, axis=-1, keepdims=True) + eps)


def rmsnorm(t, gain):
    return (_rms(t) * gain).astype(t.dtype)


def _head_rmsnorm(t, gain):
    b, s, _ = t.shape
    th = _rms(t.reshape(b, s, N_HEADS, HEAD_DIM))
    return (th.reshape(b, s, MIX_W) * gain).astype(t.dtype)


def _to_heads(t):
    b, s, _ = t.shape
    return t.reshape(b, s, N_HEADS, HEAD_DIM).transpose(0, 2, 1, 3)


def _from_heads(t):
    b, h, s, d = t.shape
    return t.transpose(0, 2, 1, 3).reshape(b, s, h * d)


def swiglu_half(x, gain, w1, w3, w2):
    h = rmsnorm(x, gain)
    return 0.5 * ((jax.nn.silu(h @ w1) * (h @ w3)) @ w2)


def rotary_tables(positions):
    inv_freq = ROPE_BASE ** (-jnp.arange(0, HEAD_DIM, 2, dtype=jnp.float32) / HEAD_DIM)
    ang = positions.astype(jnp.float32)[..., None] * inv_freq
    return jnp.cos(ang)[:, None], jnp.sin(ang)[:, None]


def apply_rotary(t, cos, sin):
    half = HEAD_DIM // 2
    t1, t2 = t[..., :half], t[..., half:]
    return jnp.concatenate([t1 * cos - t2 * sin, t1 * sin + t2 * cos], axis=-1).astype(t.dtype)


def causal_depthwise_conv(t, w, bias):
    c = t.shape[-1]
    out = lax.conv_general_dilated(t, w[:, None, :].astype(t.dtype), window_strides=(1,),
                                   padding=((CONV_W - 1, 0),),
                                   dimension_numbers=('NWC', 'WIO', 'NWC'),
                                   feature_group_count=c)
    return out + bias


def fox_attention(q, k, v, f_logit, q_gain, k_gain):
    b, h, s, d = q.shape
    nb = s // Q_BLOCK
    q = rmsnorm(q, q_gain) * (d ** -0.5)
    k = rmsnorm(k, k_gain)
    cum_f = jnp.cumsum(jax.nn.log_sigmoid(f_logit.astype(jnp.float32)), axis=1).transpose(0, 2, 1)
    q_blocks = q.reshape(b, h, nb, Q_BLOCK, d).transpose(2, 0, 1, 3, 4)
    f_blocks = cum_f.reshape(b, h, nb, Q_BLOCK).transpose(2, 0, 1, 3)
    k_pos = jnp.arange(s)

    def one_block(args):
        qb, fb, start = args
        logits = jnp.einsum('bhqd,bhkd->bhqk', qb, k).astype(jnp.float32)
        logits = logits + fb[..., :, None] - cum_f[..., None, :]
        q_pos = start + jnp.arange(Q_BLOCK)
        logits = jnp.where(k_pos[None, :] <= q_pos[:, None], logits, -jnp.inf)
        p = jax.nn.softmax(logits, axis=-1)
        return jnp.einsum('bhqk,bhkd->bhqd', p.astype(v.dtype), v)

    out = lax.map(one_block, (q_blocks, f_blocks, jnp.arange(nb, dtype=jnp.int32) * Q_BLOCK))
    return out.transpose(1, 2, 0, 3, 4).reshape(b, h, s, d)


def rwkv7_time_mix(r, k, v, w_lora, a_lora, g_lora, w0, w2, a0, a2, g2, k_k, k_a, r_k, gn_w, gn_b):
    f32 = jnp.float32
    b, s, _ = r.shape
    u = (w0 + jnp.tanh(w_lora) @ w2).astype(f32)
    decay = jnp.exp(-jnp.exp(-jax.nn.softplus(-u) - 0.5))
    a = jax.nn.sigmoid(a0 + a_lora @ a2)
    g = jax.nn.sigmoid(g_lora) @ g2

    def hd(t):
        return t.astype(f32).reshape(b, s, N_HEADS, HEAD_DIM)

    kk = hd(k * k_k)
    kk = kk * lax.rsqrt(jnp.maximum(jnp.sum(kk * kk, axis=-1, keepdims=True), 1e-12))
    k = k * (1.0 + (a - 1.0) * k_a)
    rh, kh, vh, ah, wh = hd(r), hd(k), hd(v), hd(a), hd(decay)

    def step(state, inp):
        r_t, w_t, k_t, v_t, kk_t, a_t = inp
        sa = jnp.einsum('bhvk,bhk->bhv', state, -kk_t)
        state = (state * w_t[:, :, None, :] + sa[..., :, None] * (kk_t * a_t)[:, :, None, :]
                 + v_t[..., :, None] * k_t[:, :, None, :])
        return state, jnp.einsum('bhvk,bhk->bhv', state, r_t)

    def tm(t):
        return t.transpose(1, 0, 2, 3)

    state0 = jnp.zeros((b, N_HEADS, HEAD_DIM, HEAD_DIM), f32)
    _, y = lax.scan(step, state0, (tm(rh), tm(wh), tm(kh), tm(vh), tm(kk), tm(ah)))
    y = tm(y)
    mu = jnp.mean(y, axis=-1, keepdims=True)
    var = jnp.mean(jnp.square(y - mu), axis=-1, keepdims=True)
    yn = ((y - mu) * lax.rsqrt(var + RWKV_GN_EPS)).reshape(b, s, MIX_W) * gn_w + gn_b
    bonus = jnp.sum(rh * kh * r_k, axis=-1, keepdims=True) * vh
    return ((yn + bonus.reshape(b, s, MIX_W)) * g).astype(r.dtype)


def retention(q, k, v):
    f32 = jnp.float32
    b, h, s, d = q.shape
    nc = s // CHUNK
    log_gamma = jnp.log1p(-jnp.exp2(-5.0 - jnp.arange(h, dtype=f32)))
    idx = jnp.arange(CHUNK, dtype=f32)
    intra = jnp.exp(log_gamma[:, None, None] * jnp.abs(idx[:, None] - idx[None, :]))
    q_decay = jnp.exp(log_gamma[:, None] * (idx + 1.0))
    k_decay = jnp.exp(log_gamma[:, None] * (CHUNK - 1.0 - idx))
    chunk_decay = jnp.exp(log_gamma * CHUNK)

    def chunks(t):
        return t.astype(f32).reshape(b, h, nc, CHUNK, d).transpose(2, 0, 1, 3, 4)

    def step(state, inp):
        qc, kc, vc = inp
        scores = jnp.einsum('bhid,bhjd->bhij', qc, kc) * intra
        out = (jnp.einsum('bhij,bhjv->bhiv', scores, vc)
               + jnp.einsum('bhid,bhdv->bhiv', qc * q_decay[:, :, None], state))
        state = state * chunk_decay[:, None, None] + jnp.einsum('bhjd,bhjv->bhdv', kc * k_decay[:, :, None], vc)
        return state, out

    state0 = jnp.zeros((b, h, d, d), f32)
    _, out = lax.scan(step, state0, (chunks(q), chunks(k), chunks(v)))
    return out.transpose(1, 2, 0, 3, 4).reshape(b, h, s, d)


def mlstm(q, k, v, i_logit, f_logit):
    f32 = jnp.float32
    b, h, s, d = q.shape
    nc = s // CHUNK

    def chunks(t):
        return t.astype(f32).reshape(b, h, nc, CHUNK, d).transpose(2, 0, 1, 3, 4)

    def gate_chunks(t):
        return t.astype(f32).transpose(0, 2, 1).reshape(b, h, nc, CHUNK).transpose(2, 0, 1, 3)

    log_f = gate_chunks(jax.nn.log_sigmoid(f_logit.astype(f32)))
    log_i = gate_chunks(i_logit)
    causal = jnp.tril(jnp.ones((CHUNK, CHUNK), dtype=bool))

    def step(carry, inp):
        c_mat, n_vec, m_prev = carry
        qc, kc, vc, lf, li = inp
        b_cum = jnp.cumsum(lf, axis=-1)
        log_d = jnp.where(causal, b_cum[..., :, None] - b_cum[..., None, :] + li[..., None, :], -jnp.inf)
        log_inter = b_cum + m_prev[..., None]
        m_t = jnp.maximum(log_inter, jnp.max(log_d, axis=-1))
        d_mat = jnp.exp(log_d - m_t[..., None])
        inter = jnp.exp(log_inter - m_t)
        scores = jnp.einsum('bhid,bhjd->bhij', qc, kc) * d_mat
        num = (jnp.einsum('bhij,bhjv->bhiv', scores, vc)
               + inter[..., None] * jnp.einsum('bhvk,bhik->bhiv', c_mat, qc))
        den = jnp.sum(scores, axis=-1) + inter * jnp.einsum('bhk,bhik->bhi', n_vec, qc)
        h_out = num / jnp.maximum(jnp.abs(den), jnp.exp(-m_t))[..., None]
        b_last = b_cum[..., -1]
        log_w = b_last[..., None] - b_cum + li
        m_new = jnp.maximum(b_last + m_prev, jnp.max(log_w, axis=-1))
        w = jnp.exp(log_w - m_new[..., None])
        dec = jnp.exp(b_last + m_prev - m_new)
        c_mat = dec[..., None, None] * c_mat + jnp.einsum('bhj,bhjv,bhjk->bhvk', w, vc, kc)
        n_vec = dec[..., None] * n_vec + jnp.einsum('bhj,bhjk->bhk', w, kc)
        return (c_mat, n_vec, m_new), h_out

    carry0 = (jnp.zeros((b, h, d, d), f32), jnp.zeros((b, h, d), f32), jnp.zeros((b, h), f32))
    _, out = lax.scan(step, carry0, (chunks(q), chunks(k), chunks(v), log_f, log_i))
    return out.transpose(1, 2, 0, 3, 4).reshape(b, h, s, d)


def hybrid_mixer(x, cos, sin, mix_norm, w_in, fox_f_bias, fox_q_gain, fox_k_gain,
                 rwkv_shift_mu, rwkv_w0, rwkv_w2, rwkv_a0, rwkv_a2, rwkv_g2, rwkv_k_k, rwkv_k_a,
                 rwkv_r_k, rwkv_gn_w, rwkv_gn_b, ret_gn_gain, mlstm_conv_w, mlstm_conv_b,
                 mlstm_i_bias, mlstm_f_bias, mlstm_gn_gain, w_merge_gate, merge_gate_bias,
                 w_branch, w_out):
    h = rmsnorm(x, mix_norm)
    z = h @ w_in
    z_fox, z_rwkv, z_ret, z_mlstm = _split(z, (FOX_W, RWKV_W, RET_W, MLSTM_W))

    fq, fk, fv, fo, ff = _split(z_fox, FOX_SIZES)
    y_fox = _from_heads(fox_attention(_to_heads(fq), _to_heads(fk), _to_heads(fv),
                                      ff + fox_f_bias, fox_q_gain, fox_k_gain))
    y_fox = y_fox * jax.nn.sigmoid(fo)

    z_prev = jnp.pad(z_rwkv, ((0, 0), (1, 0), (0, 0)))[:, :-1]
    z_rwkv = z_rwkv + rwkv_shift_mu * (z_prev - z_rwkv)
    rr, rk, rv, rwl, ral, rgl = _split(z_rwkv, RWKV_SIZES)
    y_rwkv = rwkv7_time_mix(rr, rk, rv, rwl, ral, rgl, rwkv_w0, rwkv_w2, rwkv_a0, rwkv_a2, rwkv_g2,
                            rwkv_k_k, rwkv_k_a, rwkv_r_k, rwkv_gn_w, rwkv_gn_b)

    tq, tk, tv, tg = _split(z_ret, RET_SIZES)
    y_ret = _from_heads(retention(apply_rotary(_to_heads(tq), cos, sin),
                                  apply_rotary(_to_heads(tk), cos, sin) * (HEAD_DIM ** -0.5),
                                  _to_heads(tv)))
    y_ret = _head_rmsnorm(y_ret, ret_gn_gain) * jax.nn.silu(tg)

    mqk, mv, mo, mi, mf = _split(z_mlstm, MLSTM_SIZES)
    mq, mk = _split(jax.nn.silu(causal_depthwise_conv(mqk, mlstm_conv_w, mlstm_conv_b)), (MIX_W, MIX_W))
    y_m = _from_heads(mlstm(_to_heads(mq), _to_heads(mk) * (HEAD_DIM ** -0.5), _to_heads(mv),
                            mi + mlstm_i_bias, mf + mlstm_f_bias))
    y_m = _head_rmsnorm(y_m * jax.nn.sigmoid(mo), mlstm_gn_gain)

    ys = jnp.stack([y_fox.astype(h.dtype), y_rwkv.astype(h.dtype), y_ret.astype(h.dtype),
                    y_m.astype(h.dtype)], axis=0)
    lifted = jnp.einsum('nbsc,ncd->nbsd', ys, w_branch)
    gates = jax.nn.sigmoid(jnp.einsum('bsd,dne->nbse', h, w_merge_gate) + merge_gate_bias[:, None, None, :])
    merged = jnp.sum(gates * lifted, axis=0)
    return (merged @ w_out).astype(x.dtype)


def setup_inputs(seed: int = 0) -> dict:
    key = jax.random.key(seed)
    ks = jax.random.split(key, 40)
    f32 = jnp.float32
    L, D, H, Dh, W = DEPTH, D_MODEL, N_HEADS, HEAD_DIM, MIX_W

    def nrm(i, shape, scale):
        return scale * jax.random.normal(ks[i], shape, f32)

    def gain(i, shape):
        return 1.0 + nrm(i, shape, 0.02)

    fbias = jnp.linspace(FGATE_BIAS_LO, FGATE_BIAS_HI, H, dtype=f32)
    return {
        'x': nrm(0, (BATCH, SEQ, D), 1.0),
        'positions': jnp.broadcast_to(jnp.arange(SEQ, dtype=jnp.int32), (BATCH, SEQ)),
        'ffn1_norm': gain(1, (L, D)),
        'ffn1_w1': nrm(2, (L, D, D_FF), D ** -0.5),
        'ffn1_w3': nrm(3, (L, D, D_FF), D ** -0.5),
        'ffn1_w2': nrm(4, (L, D_FF, D), D_FF ** -0.5),
        'mix_norm': gain(5, (L, D)),
        'w_in': nrm(6, (L, D, D_IN), D ** -0.5),
        'fox_f_bias': fbias + nrm(7, (L, H), 0.1),
        'fox_q_gain': gain(8, (L, Dh)),
        'fox_k_gain': gain(9, (L, Dh)),
        'rwkv_shift_mu': jax.random.uniform(ks[10], (L, RWKV_W), f32),
        'rwkv_w0': jax.random.uniform(ks[11], (L, W), f32, -6.5, -1.5),
        'rwkv_w2': nrm(12, (L, RWKV_W_LORA, W), 0.5 * RWKV_W_LORA ** -0.5),
        'rwkv_a0': nrm(13, (L, W), 0.1),
        'rwkv_a2': nrm(14, (L, RWKV_A_LORA, W), RWKV_A_LORA ** -0.5),
        'rwkv_g2': nrm(15, (L, RWKV_G_LORA, W), RWKV_G_LORA ** -0.5),
        'rwkv_k_k': 0.85 + nrm(16, (L, W), 0.05),
        'rwkv_k_a': 1.0 + nrm(17, (L, W), 0.05),
        'rwkv_r_k': nrm(18, (L, H, Dh), 0.1),
        'rwkv_gn_w': gain(19, (L, W)),
        'rwkv_gn_b': nrm(20, (L, W), 0.02),
        'ret_gn_gain': gain(21, (L, W)),
        'mlstm_conv_w': nrm(22, (L, CONV_W, 2 * W), CONV_W ** -0.5),
        'mlstm_conv_b': nrm(23, (L, 2 * W), 0.02),
        'mlstm_i_bias': nrm(24, (L, H), 0.1),
        'mlstm_f_bias': fbias + nrm(25, (L, H), 0.1),
        'mlstm_gn_gain': gain(26, (L, W)),
        'w_merge_gate': nrm(27, (L, D, N_BRANCH, D), D ** -0.5),
        'merge_gate_bias': nrm(28, (L, N_BRANCH, D), 0.1),
        'w_branch': nrm(29, (L, N_BRANCH, W, D), W ** -0.5),
        'w_out': nrm(30, (L, D, D), D ** -0.5),
        'ffn2_norm': gain(31, (L, D)),
        'ffn2_w1': nrm(32, (L, D, D_FF), D ** -0.5),
        'ffn2_w3': nrm(33, (L, D, D_FF), D ** -0.5),
        'ffn2_w2': nrm(34, (L, D_FF, D), D_FF ** -0.5),
    }


def reference(x, positions, ffn1_norm, ffn1_w1, ffn1_w3, ffn1_w2, mix_norm, w_in, fox_f_bias,
              fox_q_gain, fox_k_gain, rwkv_shift_mu, rwkv_w0, rwkv_w2, rwkv_a0, rwkv_a2, rwkv_g2,
              rwkv_k_k, rwkv_k_a, rwkv_r_k, rwkv_gn_w, rwkv_gn_b, ret_gn_gain, mlstm_conv_w,
              mlstm_conv_b, mlstm_i_bias, mlstm_f_bias, mlstm_gn_gain, w_merge_gate,
              merge_gate_bias, w_branch, w_out, ffn2_norm, ffn2_w1, ffn2_w3, ffn2_w2):
    cos, sin = rotary_tables(positions)
    for l in range(DEPTH):
        x = x + swiglu_half(x, ffn1_norm[l], ffn1_w1[l], ffn1_w3[l], ffn1_w2[l]).astype(x.dtype)
        x = x + hybrid_mixer(x, cos, sin, mix_norm[l], w_in[l], fox_f_bias[l], fox_q_gain[l],
                             fox_k_gain[l], rwkv_shift_mu[l], rwkv_w0[l], rwkv_w2[l], rwkv_a0[l],
                             rwkv_a2[l], rwkv_g2[l], rwkv_k_k[l], rwkv_k_a[l], rwkv_r_k[l],
                             rwkv_gn_w[l], rwkv_gn_b[l], ret_gn_gain[l], mlstm_conv_w[l],
                             mlstm_conv_b[l], mlstm_i_bias[l], mlstm_f_bias[l], mlstm_gn_gain[l],
                             w_merge_gate[l], merge_gate_bias[l], w_branch[l], w_out[l])
        x = x + swiglu_half(x, ffn2_norm[l], ffn2_w1[l], ffn2_w3[l], ffn2_w2[l]).astype(x.dtype)
    return x
```

```python
import functools
import math

import numpy as np
import jax
import jax.numpy as jnp
from jax import lax
from jax.experimental import pallas as pl
from jax.experimental.pallas import tpu as pltpu

F32 = jnp.float32
BF16 = jnp.bfloat16

N_HEADS = 8
HEAD_DIM = 64
MIX_W = N_HEADS * HEAD_DIM
N_PAIRS = N_HEADS // 2
PAIR_W = 2 * HEAD_DIM
CHUNK = 64
CONV_W = 4
LORA_W = 256
ROPE_BASE = 10000.0
NORM_EPS = 1e-6
RWKV_GN_EPS = 64e-5
NEG_BIG = -1e30
HALO = 8

Z_COLS = 8192
(C_FQ, C_FK, C_FV, C_FO, C_RR, C_RK, C_RV, C_TQ, C_TK, C_TV, C_TG,
 C_MQ, C_MK, C_MV, C_MO) = range(15)
C_RL_256 = 7680 // LORA_W
C_GATE_128 = 7936 // 128
G_FOX_F, G_ML_I, G_ML_F = 0, 8, 16

VMEM_LIMIT = 56 * 1024 * 1024


def _cparams(sem):
    return pltpu.CompilerParams(dimension_semantics=sem, vmem_limit_bytes=VMEM_LIMIT)


def _dot(a, b):
    return jnp.dot(a, b, preferred_element_type=F32)


def _dot_nt(a, b):
    return lax.dot_general(a, b, (((1,), (1,)), ((), ())), preferred_element_type=F32)


def _dot_tn(a, b):
    return lax.dot_general(a, b, (((0,), (0,)), ((), ())), preferred_element_type=F32)


def _split2(x):
    hi = x.astype(BF16)
    lo = (x - hi.astype(F32)).astype(BF16)
    return hi, lo


def _split3(x):
    hi = x.astype(BF16)
    r = x - hi.astype(F32)
    mid = r.astype(BF16)
    lo = (r - mid.astype(F32)).astype(BF16)
    return hi, mid, lo


def _dot_exact_lhs(m01, x):
    hi, mid, lo = _split3(x)
    return _dot(m01, hi) + _dot(m01, mid) + _dot(m01, lo)


def _seg_matrix(n):
    r = lax.broadcasted_iota(jnp.int32, (n, n), 0) // HEAD_DIM
    c = lax.broadcasted_iota(jnp.int32, (n, n), 1) // HEAD_DIM
    return jnp.where(r == c, 1.0, 0.0).astype(BF16)


def _seg_sum(x, seg):
    hi, lo = _split2(x)
    return _dot(hi, seg) + _dot(lo, seg)


def _sigmoid(x):
    return 1.0 / (1.0 + jnp.exp(-x))


def _silu(x):
    return x * _sigmoid(x)


def _rms_rows(x, gain):
    ms = jnp.mean(x * x, axis=-1, keepdims=True)
    return x * lax.rsqrt(ms + NORM_EPS) * gain


def _lane_is_head0(shape):
    return lax.broadcasted_iota(jnp.int32, shape, len(shape) - 1) < HEAD_DIM


def _stack_heads(x, m0):
    z = jnp.zeros_like(x)
    return jnp.concatenate([jnp.where(m0, x, z), jnp.where(m0, z, x)], axis=0)


def _unstack_rows(x_st, m0):
    return jnp.where(m0, x_st[:CHUNK], x_st[CHUNK:])


def _block_diag_mask():
    r = lax.broadcasted_iota(jnp.int32, (PAIR_W, PAIR_W), 0)
    c = lax.broadcasted_iota(jnp.int32, (PAIR_W, PAIR_W), 1)
    return (r // HEAD_DIM) == (c // HEAD_DIM), r % HEAD_DIM, c % HEAD_DIM


def _ffn_body(x_ref, g_ref, w1_ref, w3_ref, w2_ref, o_ref, h_ref):
    @pl.when(pl.program_id(1) == 0)
    def _():
        x = x_ref[...]
        h_ref[...] = _rms_rows(x, g_ref[...]).astype(BF16)
        o_ref[...] = x

    h = h_ref[...]
    a = _dot(h, w1_ref[...])
    b = _dot(h, w3_ref[...])
    g = (0.5 * _silu(a)) * b
    o_ref[...] += _dot(g.astype(BF16), w2_ref[...])


def _ffn(x, gain, w1, w3, w2, tm, tf):
    s, d = x.shape
    dff = w1.shape[1]
    return pl.pallas_call(
        _ffn_body,
        grid=(s // tm, dff // tf),
        in_specs=[
            pl.BlockSpec((tm, d), lambda i, j: (i, 0)),
            pl.BlockSpec((1, d), lambda i, j: (0, 0)),
            pl.BlockSpec((d, tf), lambda i, j: (0, j)),
            pl.BlockSpec((d, tf), lambda i, j: (0, j)),
            pl.BlockSpec((tf, d), lambda i, j: (j, 0)),
        ],
        out_specs=pl.BlockSpec((tm, d), lambda i, j: (i, 0)),
        out_shape=jax.ShapeDtypeStruct((s, d), F32),
        scratch_shapes=[pltpu.VMEM((tm, d), BF16)],
        compiler_params=_cparams(("parallel", "arbitrary")),
        name="ffn",
    )(x, gain, w1, w3, w2)


def _inproj_body(x_ref, g_ref, w_ref, o_ref, h_ref):
    @pl.when(pl.program_id(1) == 0)
    def _():
        h_ref[...] = _rms_rows(x_ref[...], g_ref[...]).astype(BF16)

    o_ref[...] = _dot(h_ref[...], w_ref[...])


def _inproj(x, gain, w, tm, tn):
    s, d = x.shape
    n = w.shape[1]
    return pl.pallas_call(
        _inproj_body,
        grid=(s // tm, n // tn),
        in_specs=[
            pl.BlockSpec((tm, d), lambda i, j: (i, 0)),
            pl.BlockSpec((1, d), lambda i, j: (0, 0)),
            pl.BlockSpec((d, tn), lambda i, j: (0, j)),
        ],
        out_specs=pl.BlockSpec((tm, tn), lambda i, j: (i, j)),
        out_shape=jax.ShapeDtypeStruct((s, n), F32),
        scratch_shapes=[pltpu.VMEM((tm, d), BF16)],
        compiler_params=_cparams(("parallel", "arbitrary")),
        name="inproj",
    )(x, gain, w)


def _merge_body(x_ref, g_ref, y0, y1, y2, y3, wg0, wg1, wg2, wg3, bg_ref, wb_ref, wo_ref,
                o_ref, h_ref):
    @pl.when(pl.program_id(1) == 0)
    def _():
        x = x_ref[...]
        h_ref[...] = _rms_rows(x, g_ref[...]).astype(BF16)
        o_ref[...] = x

    h = h_ref[...]
    merged = None
    for n, (y_ref, wg_ref) in enumerate(((y0, wg0), (y1, wg1), (y2, wg2), (y3, wg3))):
        gate = _sigmoid(_dot(h, wg_ref[...]) + bg_ref[n:n + 1, :])
        term = gate * _dot(y_ref[...], wb_ref[n])
        merged = term if merged is None else merged + term
    o_ref[...] += _dot(merged.astype(BF16), wo_ref[...])


def _merge(x, gain, ys, wg, bg, wb, wo, tm, te):
    s, d = x.shape
    nb = len(ys)
    ne = d // te
    y_specs = [pl.BlockSpec((tm, MIX_W), lambda i, j: (i, 0)) for _ in range(nb)]
    wg_specs = [pl.BlockSpec((d, te), functools.partial(lambda i, j, n: (0, n * ne + j), n=n))
                for n in range(nb)]
    return pl.pallas_call(
        _merge_body,
        grid=(s // tm, ne),
        in_specs=[
            pl.BlockSpec((tm, d), lambda i, j: (i, 0)),
            pl.BlockSpec((1, d), lambda i, j: (0, 0)),
            *y_specs, *wg_specs,
            pl.BlockSpec((nb, te), lambda i, j: (0, j)),
            pl.BlockSpec((nb, MIX_W, te), lambda i, j: (0, 0, j)),
            pl.BlockSpec((te, d), lambda i, j: (j, 0)),
        ],
        out_specs=pl.BlockSpec((tm, d), lambda i, j: (i, 0)),
        out_shape=jax.ShapeDtypeStruct((s, d), F32),
        scratch_shapes=[pltpu.VMEM((tm, d), BF16)],
        compiler_params=_cparams(("parallel", "arbitrary")),
        name="merge",
    )(x, gain, *ys, wg, wg, wg, wg, bg, wb, wo)


def _rope_body(pos_ref, inv_ref, sgn_ref, cos_ref, sin_ref):
    ang = pos_ref[...] * inv_ref[...]
    cos_ref[...] = jnp.cos(ang)
    sin_ref[...] = jnp.sin(ang) * sgn_ref[...]


def _rope_tables(pos_col, tm):
    s = pos_col.shape[0]
    half = HEAD_DIM // 2
    inv = ROPE_BASE ** (-jnp.arange(0, HEAD_DIM, 2, dtype=F32) / HEAD_DIM)
    inv = jnp.tile(inv, PAIR_W // half)[None, :]
    sgn = jnp.tile(jnp.concatenate([-jnp.ones((half,), F32), jnp.ones((half,), F32)]), 2)[None, :]
    return pl.pallas_call(
        _rope_body,
        grid=(s // tm,),
        in_specs=[pl.BlockSpec((tm, 1), lambda i: (i, 0)),
                  pl.BlockSpec((1, PAIR_W), lambda i: (0, 0)),
                  pl.BlockSpec((1, PAIR_W), lambda i: (0, 0))],
        out_specs=[pl.BlockSpec((tm, PAIR_W), lambda i: (i, 0))] * 2,
        out_shape=[jax.ShapeDtypeStruct((s, PAIR_W), F32)] * 2,
        compiler_params=_cparams(("parallel",)),
        name="rope_tables",
    )(pos_col, inv, sgn)


def _gates_body(z_ref, bias_ref, g_ref, gt_ref, gtc_ref, carry_ref):
    tm = z_ref.shape[0]

    @pl.when(pl.program_id(0) == 0)
    def _():
        carry_ref[...] = jnp.zeros_like(carry_ref)

    v = z_ref[...] + bias_ref[...]
    ls = jnp.minimum(v, 0.0) - jnp.log1p(jnp.exp(-jnp.abs(v)))
    r = lax.broadcasted_iota(jnp.int32, (tm, tm), 0)
    c = lax.broadcasted_iota(jnp.int32, (tm, tm), 1)
    tri = jnp.where(c <= r, 1.0, 0.0).astype(BF16)
    tri_chunk = jnp.where((c <= r) & ((r // CHUNK) == (c // CHUNK)), 1.0, 0.0).astype(BF16)
    run = _dot_exact_lhs(tri, ls) + carry_ref[...]
    loc = _dot_exact_lhs(tri_chunk, ls)
    lane = lax.broadcasted_iota(jnp.int32, (1, 128), 1)
    out = jnp.where(lane < G_ML_I, run, jnp.where(lane < G_ML_F, v, loc))
    carry_ref[...] = run[tm - 1:tm, :]
    g_ref[...] = out
    out_t = out.T
    gt_ref[...] = out_t
    for ci in range(tm // CHUNK):
        gtc_ref[ci] = out_t[:32, ci * CHUNK:(ci + 1) * CHUNK]


def _gates(z, bias, tm):
    s = z.shape[0]
    return pl.pallas_call(
        _gates_body,
        grid=(s // tm,),
        in_specs=[pl.BlockSpec((tm, 128), lambda i: (i, C_GATE_128)),
                  pl.BlockSpec((1, 128), lambda i: (0, 0))],
        out_specs=[pl.BlockSpec((tm, 128), lambda i: (i, 0)),
                   pl.BlockSpec((128, tm), lambda i: (0, i)),
                   pl.BlockSpec((tm // CHUNK, 32, CHUNK), lambda i: (i, 0, 0))],
        out_shape=[jax.ShapeDtypeStruct((s, 128), F32),
                   jax.ShapeDtypeStruct((128, s), F32),
                   jax.ShapeDtypeStruct((s // CHUNK, 32, CHUNK), F32)],
        scratch_shapes=[pltpu.VMEM((1, 128), F32)],
        compiler_params=_cparams(("arbitrary",)),
        name="gates",
    )(z, bias)


def _fox_prep_body(zq_ref, zk_ref, zv_ref, qg_ref, kg_ref, q_ref, k_ref, v_ref):
    seg = _seg_matrix(MIX_W)
    q = zq_ref[...]
    k = zk_ref[...]
    q_ms = _seg_sum(q * q, seg) * (1.0 / HEAD_DIM)
    k_ms = _seg_sum(k * k, seg) * (1.0 / HEAD_DIM)
    q_ref[...] = (q * lax.rsqrt(q_ms + NORM_EPS) * qg_ref[...] * (HEAD_DIM ** -0.5)).astype(BF16)
    k_ref[...] = (k * lax.rsqrt(k_ms + NORM_EPS) * kg_ref[...]).astype(BF16)
    v_ref[...] = zv_ref[...].astype(BF16)


def _fox_prep(z, q_gain, k_gain, tm):
    s = z.shape[0]
    zspec = lambda c: pl.BlockSpec((tm, MIX_W), lambda i: (i, c))
    return pl.pallas_call(
        _fox_prep_body,
        grid=(s // tm,),
        in_specs=[zspec(C_FQ), zspec(C_FK), zspec(C_FV),
                  pl.BlockSpec((1, MIX_W), lambda i: (0, 0)),
                  pl.BlockSpec((1, MIX_W), lambda i: (0, 0))],
        out_specs=[pl.BlockSpec((tm, MIX_W), lambda i: (i, 0))] * 3,
        out_shape=[jax.ShapeDtypeStruct((s, MIX_W), BF16)] * 3,
        compiler_params=_cparams(("parallel",)),
        name="fox_prep",
    )(z, z, z, q_gain, k_gain)


def _fox_body(it_ref, jt_ref, q_ref, k_ref, v_ref, fc_ref, fr_ref, og_ref, o_ref,
              m_ref, l_ref, acc_ref, fcol_ref):
    p = pl.program_id(0)
    t = pl.program_id(1)
    i = it_ref[t]
    j = jt_ref[t]
    tq, tk = q_ref.shape[0], k_ref.shape[0]
    lane = lax.broadcasted_iota(jnp.int32, (1, PAIR_W), 1)
    m0 = lane < HEAD_DIM

    @pl.when(j == 0)
    def _():
        m_ref[...] = jnp.full_like(m_ref, NEG_BIG)
        l_ref[...] = jnp.zeros_like(l_ref)
        acc_ref[...] = jnp.zeros_like(acc_ref)
        fc = fc_ref[...]
        for e in range(2):
            fcol_ref[e] = jnp.sum(jnp.where(lane == G_FOX_F + 2 * p + e, fc, 0.0),
                                  axis=1, keepdims=True)

    q = q_ref[...]
    k = k_ref[...]
    v = v_ref[...]
    row = lax.broadcasted_iota(jnp.int32, (tq, tk), 0) + i * tq
    col = lax.broadcasted_iota(jnp.int32, (tq, tk), 1) + j * tk
    causal = col <= row
    zq = jnp.zeros_like(q)
    pvs, alphas = [], []
    for e in range(2):
        qm = jnp.where(m0, q, zq) if e == 0 else jnp.where(m0, zq, q)
        s = _dot_nt(qm, k)
        fr = fr_ref[pl.ds(G_FOX_F + 2 * p + e, 1), :]
        s = jnp.where(causal, s + fcol_ref[e] - fr, NEG_BIG)
        m_prev = m_ref[e]
        m_new = jnp.maximum(m_prev, jnp.max(s, axis=1, keepdims=True))
        alpha = jnp.exp(m_prev - m_new)
        pr = jnp.exp(s - m_new)
        l_ref[e] = alpha * l_ref[e] + jnp.sum(pr, axis=1, keepdims=True)
        m_ref[e] = m_new
        pvs.append(_dot(pr.astype(BF16), v))
        alphas.append(alpha)
    acc_ref[...] = jnp.where(m0, alphas[0], alphas[1]) * acc_ref[...] + jnp.where(m0, pvs[0], pvs[1])

    @pl.when(j == i)
    def _():
        l_pair = jnp.where(m0, l_ref[0], l_ref[1])
        o_ref[...] = (acc_ref[...] / l_pair * _sigmoid(og_ref[...])).astype(o_ref.dtype)


def _fox_attention(q, k, v, g, gt, z, tq):
    s = q.shape[0]
    nq = s // tq
    ii, jj = np.tril_indices(nq)
    it = jnp.asarray(ii, jnp.int32)
    jt = jnp.asarray(jj, jnp.int32)
    fo_blk = C_FO * (MIX_W // PAIR_W)
    grid_spec = pltpu.PrefetchScalarGridSpec(
        num_scalar_prefetch=2,
        grid=(N_PAIRS, len(ii)),
        in_specs=[
            pl.BlockSpec((tq, PAIR_W), lambda p, t, it, jt: (it[t], p)),
            pl.BlockSpec((tq, PAIR_W), lambda p, t, it, jt: (jt[t], p)),
            pl.BlockSpec((tq, PAIR_W), lambda p, t, it, jt: (jt[t], p)),
            pl.BlockSpec((tq, 128), lambda p, t, it, jt: (it[t], 0)),
            pl.BlockSpec((8, tq), lambda p, t, it, jt: (0, jt[t])),
            pl.BlockSpec((tq, PAIR_W), lambda p, t, it, jt: (it[t], fo_blk + p)),
        ],
        out_specs=pl.BlockSpec((tq, PAIR_W), lambda p, t, it, jt: (it[t], p)),
        scratch_shapes=[pltpu.VMEM((2, tq, 1), F32), pltpu.VMEM((2, tq, 1), F32),
                        pltpu.VMEM((tq, PAIR_W), F32), pltpu.VMEM((2, tq, 1), F32)],
    )
    return pl.pallas_call(
        _fox_body,
        grid_spec=grid_spec,
        out_shape=jax.ShapeDtypeStruct((s, MIX_W), BF16),
        compiler_params=_cparams(("parallel", "arbitrary")),
        name="fox_attention",
    )(it, jt, q, k, v, g, gt, z)


def _retention_body(zq_ref, zk_ref, zv_ref, zg_ref, cos_ref, sin_ref, intra_ref, qd_ref, kd_ref,
                    cd_ref, gain_ref, o_ref, st_ref, q_scr, k_scr, y_scr):
    tm = zq_ref.shape[0]

    @pl.when(pl.program_id(0) == 0)
    def _():
        st_ref[...] = jnp.zeros_like(st_ref)

    lane = lax.broadcasted_iota(jnp.int32, (1, PAIR_W), 1)
    m0 = lane < HEAD_DIM
    first_half = (lane & (HEAD_DIM // 2)) == 0
    cos = cos_ref[...]
    sin = sin_ref[...]
    for p in range(N_PAIRS):
        cs = slice(p * PAIR_W, (p + 1) * PAIR_W)
        for src, dst, scale in ((zq_ref, q_scr, 1.0), (zk_ref, k_scr, HEAD_DIM ** -0.5)):
            x = src[:, cs]
            swapped = jnp.where(first_half, pltpu.roll(x, PAIR_W - HEAD_DIM // 2, 1),
                                pltpu.roll(x, HEAD_DIM // 2, 1))
            dst[:, cs] = (x * cos + swapped * sin) * scale
    bd, _, _ = _block_diag_mask()

    def chunk(c, carry):
        rows = pl.ds(pl.multiple_of(c * CHUNK, CHUNK), CHUNK)
        for p in range(N_PAIRS):
            cs = slice(p * PAIR_W, (p + 1) * PAIR_W)
            q = q_scr[rows, cs]
            k = k_scr[rows, cs]
            vb = zv_ref[rows, cs].astype(BF16)
            sc = _dot_nt(_stack_heads(q, m0).astype(BF16), k.astype(BF16)) * intra_ref[p]
            o_intra = _unstack_rows(_dot(sc.astype(BF16), vb), m0)
            st = st_ref[p]
            o_inter = _dot((q * qd_ref[p]).astype(BF16), st.astype(BF16))
            y_scr[rows, cs] = o_intra + o_inter
            upd = _dot_tn((k * kd_ref[p]).astype(BF16), vb)
            st_ref[p] = st * cd_ref[p] + jnp.where(bd, upd, 0.0)
        return carry

    lax.fori_loop(0, tm // CHUNK, chunk, 0)
    y = y_scr[...]
    ms = _seg_sum(y * y, _seg_matrix(MIX_W)) * (1.0 / HEAD_DIM)
    o_ref[...] = (y * lax.rsqrt(ms + NORM_EPS) * gain_ref[...] * _silu(zg_ref[...])).astype(o_ref.dtype)


def _retention_tables():
    hh = jnp.arange(N_HEADS, dtype=F32)
    log_gamma = jnp.log1p(-jnp.exp2(-5.0 - hh))
    idx = jnp.arange(CHUNK, dtype=F32)
    intra = jnp.exp(log_gamma[:, None, None] * jnp.abs(idx[:, None] - idx[None, :]))
    q_decay = jnp.exp(log_gamma[:, None] * (idx + 1.0))
    k_decay = jnp.exp(log_gamma[:, None] * (CHUNK - 1.0 - idx))
    chunk_decay = jnp.exp(log_gamma * CHUNK)
    intra_st = intra.reshape(N_PAIRS, 2 * CHUNK, CHUNK)
    to_pair = lambda t: jnp.repeat(t.reshape(N_PAIRS, 2, CHUNK).transpose(0, 2, 1), HEAD_DIM, axis=2)
    cd = jnp.repeat(chunk_decay.reshape(N_PAIRS, 1, 2), HEAD_DIM, axis=2)
    return intra_st, to_pair(q_decay), to_pair(k_decay), cd


def _retention(z, cos, sin, gain, tm):
    s = z.shape[0]
    intra_st, qd, kd, cd = _retention_tables()
    zspec = lambda c: pl.BlockSpec((tm, MIX_W), lambda i: (i, c))
    full = lambda a: pl.BlockSpec(a.shape, lambda i: (0,) * a.ndim)
    return pl.pallas_call(
        _retention_body,
        grid=(s // tm,),
        in_specs=[zspec(C_TQ), zspec(C_TK), zspec(C_TV), zspec(C_TG),
                  pl.BlockSpec((tm, PAIR_W), lambda i: (i, 0)),
                  pl.BlockSpec((tm, PAIR_W), lambda i: (i, 0)),
                  full(intra_st), full(qd), full(kd), full(cd),
                  pl.BlockSpec((1, MIX_W), lambda i: (0, 0))],
        out_specs=pl.BlockSpec((tm, MIX_W), lambda i: (i, 0)),
        out_shape=jax.ShapeDtypeStruct((s, MIX_W), BF16),
        scratch_shapes=[pltpu.VMEM((N_PAIRS, PAIR_W, PAIR_W), F32),
                        pltpu.VMEM((tm, MIX_W), F32), pltpu.VMEM((tm, MIX_W), F32),
                        pltpu.VMEM((tm, MIX_W), F32)],
        compiler_params=_cparams(("arbitrary",)),
        name="retention",
    )(z, z, z, z, cos, sin, intra_st, qd, kd, cd, gain)


def _load_with_halo(ext_ref, cur_ref, halo_ref, first):
    halo = halo_ref[...]
    ext_ref[0:HALO, :] = jnp.where(first, jnp.zeros_like(halo), halo)
    ext_ref[HALO:, :] = cur_ref[...]


def _mlstm_body(zq_ref, zk_ref, hq_ref, hk_ref, zv_ref, zo_ref, g_ref, gtc_ref, cw_ref, cb_ref,
                gain_ref, o_ref, c_ref, n_ref, m_ref, ext_scr, q_scr, k_scr, y_scr):
    tm = zq_ref.shape[0]
    first = pl.program_id(0) == 0

    @pl.when(first)
    def _():
        c_ref[...] = jnp.zeros_like(c_ref)
        n_ref[...] = jnp.zeros_like(n_ref)
        m_ref[...] = jnp.zeros_like(m_ref)

    for part, (cur, halo, dst, scale) in enumerate(((zq_ref, hq_ref, q_scr, 1.0),
                                                    (zk_ref, hk_ref, k_scr, HEAD_DIM ** -0.5))):
        _load_with_halo(ext_scr, cur, halo, first)
        cols = slice(part * MIX_W, (part + 1) * MIX_W)
        acc = cb_ref[:, cols]
        for tap in range(CONV_W):
            off = HALO - (CONV_W - 1) + tap
            acc = acc + cw_ref[tap:tap + 1, cols] * ext_scr[off:off + tm, :]
        dst[...] = _silu(acc) * scale

    lane = lax.broadcasted_iota(jnp.int32, (1, PAIR_W), 1)
    m0 = lane < HEAD_DIM
    bd, _, _ = _block_diag_mask()
    ri = lax.broadcasted_iota(jnp.int32, (CHUNK, CHUNK), 0)
    ci = lax.broadcasted_iota(jnp.int32, (CHUNK, CHUNK), 1)
    causal = ci <= ri

    def chunk(c, carry):
        rows = pl.ds(pl.multiple_of(c * CHUNK, CHUNK), CHUNK)
        g = g_ref[rows, :]
        gt = gtc_ref[c]
        for p in range(N_PAIRS):
            cs = slice(p * PAIR_W, (p + 1) * PAIR_W)
            q = q_scr[rows, cs]
            k = k_scr[rows, cs]
            vb = zv_ref[rows, cs].astype(BF16)
            m_pair = m_ref[p]
            d_mats, inters, m_ts, ws, decs, m_news = [], [], [], [], [], []
            for e in range(2):
                h = 2 * p + e
                b_c = g[:, G_ML_F + h:G_ML_F + h + 1]
                li_c = g[:, G_ML_I + h:G_ML_I + h + 1]
                b_r = gt[G_ML_F + h:G_ML_F + h + 1, :]
                li_r = gt[G_ML_I + h:G_ML_I + h + 1, :]
                m_prev = m_pair[:, e * HEAD_DIM:e * HEAD_DIM + 1]
                log_d = jnp.where(causal, b_c - b_r + li_r, NEG_BIG)
                log_inter = b_c + m_prev
                m_t = jnp.maximum(log_inter, jnp.max(log_d, axis=1, keepdims=True))
                d_mats.append(jnp.exp(log_d - m_t))
                inters.append(jnp.exp(log_inter - m_t))
                m_ts.append(m_t)
                b_last = b_c[CHUNK - 1:CHUNK, :]
                log_w = b_last - b_c + li_c
                m_new = jnp.maximum(b_last + m_prev, jnp.max(log_w, axis=0, keepdims=True))
                ws.append(jnp.exp(log_w - m_new))
                decs.append(jnp.exp(b_last + m_prev - m_new))
                m_news.append(m_new)
            pair = lambda t0, t1: jnp.where(m0, t0, t1)
            sc = _dot_nt(_stack_heads(q, m0).astype(BF16), k.astype(BF16))
            sc = sc * jnp.concatenate(d_mats, axis=0)
            num = _unstack_rows(_dot(sc.astype(BF16), vb), m0)
            den_intra = jnp.sum(sc, axis=1, keepdims=True)
            cst = c_ref[p]
            nst = n_ref[p]
            inter_pair = pair(inters[0], inters[1])
            num = num + inter_pair * _dot(q.astype(BF16), cst.astype(BF16))
            qn = q * nst
            zqn = jnp.zeros_like(qn)
            qn0 = jnp.sum(jnp.where(m0, qn, zqn), axis=1, keepdims=True)
            qn1 = jnp.sum(jnp.where(m0, zqn, qn), axis=1, keepdims=True)
            den0 = den_intra[:CHUNK] + inters[0] * qn0
            den1 = den_intra[CHUNK:] + inters[1] * qn1
            denom = pair(jnp.maximum(jnp.abs(den0), jnp.exp(-m_ts[0])),
                         jnp.maximum(jnp.abs(den1), jnp.exp(-m_ts[1])))
            y_scr[rows, cs] = num / denom
            kw = k * pair(ws[0], ws[1])
            dec_pair = pair(decs[0], decs[1])
            c_ref[p] = dec_pair * cst + jnp.where(bd, _dot_tn(kw.astype(BF16), vb), 0.0)
            n_ref[p] = dec_pair * nst + jnp.sum(kw, axis=0, keepdims=True)
            m_ref[p] = pair(m_news[0], m_news[1])
        return carry

    lax.fori_loop(0, tm // CHUNK, chunk, 0)
    y = y_scr[...] * _sigmoid(zo_ref[...])
    ms = _seg_sum(y * y, _seg_matrix(MIX_W)) * (1.0 / HEAD_DIM)
    o_ref[...] = (y * lax.rsqrt(ms + NORM_EPS) * gain_ref[...]).astype(o_ref.dtype)


def _halo_spec(tm, width, c):
    step = tm // HALO
    return pl.BlockSpec((HALO, width), lambda i: (jnp.maximum(i * step - 1, 0), c))


def _mlstm(z, g, gtc, conv_w, conv_b, gain, tm):
    s = z.shape[0]
    zspec = lambda c: pl.BlockSpec((tm, MIX_W), lambda i: (i, c))
    return pl.pallas_call(
        _mlstm_body,
        grid=(s // tm,),
        in_specs=[zspec(C_MQ), zspec(C_MK), _halo_spec(tm, MIX_W, C_MQ), _halo_spec(tm, MIX_W, C_MK),
                  zspec(C_MV), zspec(C_MO),
                  pl.BlockSpec((tm, 128), lambda i: (i, 0)),
                  pl.BlockSpec((tm // CHUNK, 32, CHUNK), lambda i: (i, 0, 0)),
                  pl.BlockSpec((CONV_W, 2 * MIX_W), lambda i: (0, 0)),
                  pl.BlockSpec((1, 2 * MIX_W), lambda i: (0, 0)),
                  pl.BlockSpec((1, MIX_W), lambda i: (0, 0))],
        out_specs=pl.BlockSpec((tm, MIX_W), lambda i: (i, 0)),
        out_shape=jax.ShapeDtypeStruct((s, MIX_W), BF16),
        scratch_shapes=[pltpu.VMEM((N_PAIRS, PAIR_W, PAIR_W), F32),
                        pltpu.VMEM((N_PAIRS, 1, PAIR_W), F32),
                        pltpu.VMEM((N_PAIRS, 1, PAIR_W), F32),
                        pltpu.VMEM((tm + HALO, MIX_W), F32),
                        pltpu.VMEM((tm, MIX_W), F32), pltpu.VMEM((tm, MIX_W), F32),
                        pltpu.VMEM((tm, MIX_W), F32)],
        compiler_params=_cparams(("arbitrary",)),
        name="mlstm",
    )(z, z, z, z, z, z, g, gtc, conv_w, conv_b, gain)


def _rwkv_body(zr_ref, zk_ref, zv_ref, zl_ref, hr_ref, hk_ref, hv_ref, hl_ref, mu_ref, mul_ref,
               wl_ref, w0_ref, a0_ref, kk_ref, ka_ref, rk_ref, gnw_ref, gnb_ref, o_ref,
               h_ref, ext_scr, extl_scr, rt_scr, kt_scr, kb_scr, bb_scr, kh_scr, bh_scr,
               gam_scr, v_scr, y_scr):
    tm = zr_ref.shape[0]
    first = pl.program_id(0) == 0

    @pl.when(first)
    def _():
        h_ref[...] = jnp.zeros_like(h_ref)

    def shifted(ext, cur_ref, halo_ref, mu):
        _load_with_halo(ext, cur_ref, halo_ref, first)
        cur = ext[HALO:HALO + tm, :]
        prev = ext[HALO - 1:HALO - 1 + tm, :]
        return cur + mu * (prev - cur)

    r = shifted(ext_scr, zr_ref, hr_ref, mu_ref[:, 0:MIX_W])
    k = shifted(ext_scr, zk_ref, hk_ref, mu_ref[:, MIX_W:2 * MIX_W])
    v = shifted(ext_scr, zv_ref, hv_ref, mu_ref[:, 2 * MIX_W:3 * MIX_W])
    lora_in = shifted(extl_scr, zl_ref, hl_ref, mul_ref[...])
    lane_l = lax.broadcasted_iota(jnp.int32, (1, LORA_W), 1)
    act = jnp.where(lane_l < 64, jnp.tanh(lora_in), jnp.where(lane_l < 128, lora_in, _sigmoid(lora_in)))
    lora = _dot(act.astype(BF16), wl_ref[...])
    logw = -_sigmoid(w0_ref[...] + lora[:, 0:MIX_W]) * math.exp(-0.5)
    a = _sigmoid(a0_ref[...] + lora[:, MIX_W:2 * MIX_W])
    gate = lora[:, 2 * MIX_W:3 * MIX_W]

    seg = _seg_matrix(MIX_W)
    kk = k * kk_ref[...]
    kk = kk * lax.rsqrt(jnp.maximum(_seg_sum(kk * kk, seg), 1e-12))
    k2 = k * (1.0 + (a - 1.0) * ka_ref[...])
    b = kk * a
    bonus = _seg_sum(r * k2 * rk_ref[...], seg) * v

    ri = lax.broadcasted_iota(jnp.int32, (tm, tm), 0)
    ci = lax.broadcasted_iota(jnp.int32, (tm, tm), 1)
    same = (ri // CHUNK) == (ci // CHUNK)
    lower = jnp.where(same & (ci <= ri), 1.0, 0.0).astype(BF16)
    upper = jnp.where(same & (ci > ri), 1.0, 0.0).astype(BF16)
    cl = _dot_exact_lhs(lower, logw)
    cs = _dot_exact_lhs(upper, logw)
    e_cl = jnp.exp(cl)
    e_ncl = jnp.exp(-cl)
    e_cs = jnp.exp(cs)
    rt_scr[...] = r * e_cl
    kt_scr[...] = kk * jnp.exp(cl - logw)
    kb_scr[...] = k2 * e_ncl
    bb_scr[...] = b * e_ncl
    kh_scr[...] = k2 * e_cs
    bh_scr[...] = b * e_cs
    gam_scr[...] = jnp.exp(cl + cs)
    v_scr[...] = v

    lane = lax.broadcasted_iota(jnp.int32, (1, PAIR_W), 1)
    m0 = lane < HEAD_DIM
    bd, rr, cc = _block_diag_mask()
    strict = bd & (cc < rr)
    incl = bd & (cc <= rr)
    eye = bd & (cc == rr)
    eye_f = jnp.where(eye, 1.0, 0.0)

    def chunk(c, carry):
        rows = pl.ds(pl.multiple_of(c * CHUNK, CHUNK), CHUNK)
        first8 = pl.ds(pl.multiple_of(c * CHUNK, CHUNK), 8)
        for p in range(N_PAIRS):
            cols = slice(p * PAIR_W, (p + 1) * PAIR_W)
            rt_st = _stack_heads(rt_scr[rows, cols], m0)
            kt_st = _stack_heads(kt_scr[rows, cols], m0)
            v_st = _stack_heads(v_scr[rows, cols], m0).astype(BF16)
            kh_st = _stack_heads(kh_scr[rows, cols], m0).astype(BF16)
            bh_st = _stack_heads(bh_scr[rows, cols], m0).astype(BF16)
            bb = bb_scr[rows, cols].astype(BF16)
            kb = kb_scr[rows, cols].astype(BF16)
            lhs = jnp.concatenate([kt_st, rt_st], axis=0).astype(BF16)
            g_b = _dot_nt(lhs, jnp.concatenate([bb, bb], axis=0))
            g_k = _dot_nt(lhs, jnp.concatenate([kb, kb], axis=0))
            a_bd = jnp.where(strict, g_b[:PAIR_W], 0.0)
            q_bd = jnp.where(incl, g_b[PAIR_W:], 0.0).astype(BF16)
            b_bd = jnp.where(strict, g_k[:PAIR_W], 0.0).astype(BF16)
            p_bd = jnp.where(incl, g_k[PAIR_W:], 0.0).astype(BF16)
            x = -a_bd
            t_inv = eye_f + x
            for _ in range(5):
                xb = x.astype(BF16)
                x = _dot(xb, xb)
                t_inv = t_inv + _dot(t_inv.astype(BF16), x.astype(BF16))
            tb = t_inv.astype(BF16)
            kt2 = _dot(tb, kt_st.astype(BF16))
            w1 = _dot(tb, _dot(b_bd, v_st).astype(BF16))
            kt2b = kt2.astype(BF16)
            w1b = w1.astype(BF16)
            y1 = rt_st - _dot(q_bd, kt2b)
            y0 = _dot(p_bd, v_st) - _dot(q_bd, w1b)
            gam = gam_scr[first8, cols][0:1]
            m_mat = jnp.where(eye, gam, 0.0) - _dot_tn(bh_st, kt2b)
            n_mat = _dot_tn(kh_st, v_st) - _dot_tn(bh_st, w1b)
            hb = h_ref[p]
            hbb = hb.astype(BF16)
            y_st = _dot(y1.astype(BF16), hbb) + y0
            y_scr[rows, cols] = y_st[:CHUNK] + y_st[CHUNK:]
            h_ref[p] = _dot(m_mat.astype(BF16), hbb) + n_mat
        return carry

    lax.fori_loop(0, tm // CHUNK, chunk, 0)

    y = y_scr[...]
    mean = _seg_sum(y, seg) * (1.0 / HEAD_DIM)
    yc = y - mean
    var = _seg_sum(yc * yc, seg) * (1.0 / HEAD_DIM)
    yn = yc * lax.rsqrt(var + RWKV_GN_EPS) * gnw_ref[...] + gnb_ref[...]
    o_ref[...] = ((yn + bonus) * gate).astype(o_ref.dtype)


def _rwkv(z, mu, mu_l, w_lora, w0, a0, k_k, k_a, r_k, gn_w, gn_b, tm):
    s = z.shape[0]
    zspec = lambda c: pl.BlockSpec((tm, MIX_W), lambda i: (i, c))
    row = lambda w: pl.BlockSpec((1, w), lambda i: (0, 0))
    big = pltpu.VMEM((tm, MIX_W), F32)
    return pl.pallas_call(
        _rwkv_body,
        grid=(s // tm,),
        in_specs=[zspec(C_RR), zspec(C_RK), zspec(C_RV),
                  pl.BlockSpec((tm, LORA_W), lambda i: (i, C_RL_256)),
                  _halo_spec(tm, MIX_W, C_RR), _halo_spec(tm, MIX_W, C_RK), _halo_spec(tm, MIX_W, C_RV),
                  _halo_spec(tm, LORA_W, C_RL_256),
                  row(3 * MIX_W), row(LORA_W),
                  pl.BlockSpec((LORA_W, 3 * MIX_W), lambda i: (0, 0)),
                  row(MIX_W), row(MIX_W), row(MIX_W), row(MIX_W), row(MIX_W), row(MIX_W), row(MIX_W)],
        out_specs=pl.BlockSpec((tm, MIX_W), lambda i: (i, 0)),
        out_shape=jax.ShapeDtypeStruct((s, MIX_W), BF16),
        scratch_shapes=[pltpu.VMEM((N_PAIRS, PAIR_W, PAIR_W), F32),
                        pltpu.VMEM((tm + HALO, MIX_W), F32), pltpu.VMEM((tm + HALO, LORA_W), F32),
                        big, big, big, big, big, big, big, big, big],
        compiler_params=_cparams(("arbitrary",)),
        name="rwkv7",
    )(z, z, z, z, z, z, z, z, mu, mu_l, w_lora, w0, a0, k_k, k_a, r_k, gn_w, gn_b)


def _w_in_column_map():
    fox, rwkv, ret, ml = 0, 2056, 3848, 5896
    src = np.full((Z_COLS,), -1, np.int64)

    def put(dst, start, n):
        src[dst:dst + n] = np.arange(start, start + n)

    for blk, start in ((C_FQ, fox), (C_FK, fox + 512), (C_FV, fox + 1024), (C_FO, fox + 1536),
                       (C_RR, rwkv), (C_RK, rwkv + 512), (C_RV, rwkv + 1024),
                       (C_TQ, ret), (C_TK, ret + 512), (C_TV, ret + 1024), (C_TG, ret + 1536),
                       (C_MQ, ml), (C_MK, ml + 512), (C_MV, ml + 1024), (C_MO, ml + 1536)):
        put(blk * MIX_W, start, MIX_W)
    put(C_RL_256 * LORA_W, rwkv + 1536, LORA_W)
    gate0 = C_GATE_128 * 128
    put(gate0 + G_FOX_F, fox + 2048, N_HEADS)
    put(gate0 + G_ML_I, ml + 2048, N_HEADS)
    put(gate0 + G_ML_F, ml + 2056, N_HEADS)
    return src


def _permute_w_in(w_in):
    src = _w_in_column_map()
    valid = src >= 0
    cols = jnp.take(w_in, jnp.asarray(np.where(valid, src, 0)), axis=1)
    return jnp.where(jnp.asarray(valid)[None, :], cols, 0.0).astype(BF16)


def kernel(x, positions, ffn1_norm, ffn1_w1, ffn1_w3, ffn1_w2, mix_norm, w_in, fox_f_bias, fox_q_gain, fox_k_gain, rwkv_shift_mu, rwkv_w0, rwkv_w2, rwkv_a0, rwkv_a2, rwkv_g2, rwkv_k_k, rwkv_k_a, rwkv_r_k, rwkv_gn_w, rwkv_gn_b, ret_gn_gain, mlstm_conv_w, mlstm_conv_b, mlstm_i_bias, mlstm_f_bias, mlstm_gn_gain, w_merge_gate, merge_gate_bias, w_branch, w_out, ffn2_norm, ffn2_w1, ffn2_w3, ffn2_w2):
    batch, seq, d = x.shape
    depth = w_in.shape[0]
    dff = ffn1_w1.shape[-1]
    pick = lambda full, want: want if full % want == 0 else full
    tm_dense = pick(seq, 512)
    tf = pick(dff, 512)
    te = pick(d, 256)
    tn = pick(Z_COLS, 1024)
    tm_row = pick(seq, 512)
    tm_seq = pick(seq, 256)
    tq = pick(seq, 512)
    row = lambda t: t.reshape(1, -1).astype(F32)

    outs = []
    for bi in range(batch):
        xb = x[bi]
        pos_col = positions[bi].astype(F32).reshape(seq, 1)
        cos, sin = _rope_tables(pos_col, tm_row)
        for l in range(depth):
            xb = _ffn(xb, row(ffn1_norm[l]), ffn1_w1[l].astype(BF16), ffn1_w3[l].astype(BF16),
                      ffn1_w2[l].astype(BF16), tm_dense, tf)
            z = _inproj(xb, row(mix_norm[l]), _permute_w_in(w_in[l]), tm_dense, tn)

            gate_bias = jnp.zeros((128,), F32)
            gate_bias = gate_bias.at[G_FOX_F:G_FOX_F + N_HEADS].set(fox_f_bias[l])
            gate_bias = gate_bias.at[G_ML_I:G_ML_I + N_HEADS].set(mlstm_i_bias[l])
            gate_bias = gate_bias.at[G_ML_F:G_ML_F + N_HEADS].set(mlstm_f_bias[l])
            g, gt, gtc = _gates(z, gate_bias[None, :], tm_row)

            fq, fk, fv = _fox_prep(z, row(jnp.tile(fox_q_gain[l], N_HEADS)),
                                   row(jnp.tile(fox_k_gain[l], N_HEADS)), tm_row)
            y_fox = _fox_attention(fq, fk, fv, g, gt, z, tq)

            mu = rwkv_shift_mu[l]
            w_lora = jnp.zeros((LORA_W, 3 * MIX_W), F32)
            w_lora = w_lora.at[0:64, 0:MIX_W].set(rwkv_w2[l])
            w_lora = w_lora.at[64:128, MIX_W:2 * MIX_W].set(rwkv_a2[l])
            w_lora = w_lora.at[128:256, 2 * MIX_W:3 * MIX_W].set(rwkv_g2[l])
            y_rwkv = _rwkv(z, row(mu[:3 * MIX_W]), row(mu[3 * MIX_W:]), w_lora.astype(BF16),
                           row(rwkv_w0[l]), row(rwkv_a0[l]), row(rwkv_k_k[l]), row(rwkv_k_a[l]),
                           row(rwkv_r_k[l]), row(rwkv_gn_w[l]), row(rwkv_gn_b[l]), tm_seq)

            y_ret = _retention(z, cos, sin, row(ret_gn_gain[l]), tm_seq)

            y_m = _mlstm(z, g, gtc, mlstm_conv_w[l].astype(F32), row(mlstm_conv_b[l]),
                         row(mlstm_gn_gain[l]), tm_seq)

            xb = _merge(xb, row(mix_norm[l]), (y_fox, y_rwkv, y_ret, y_m),
                        w_merge_gate[l].reshape(d, -1).astype(BF16), merge_gate_bias[l].astype(F32),
                        w_branch[l].astype(BF16), w_out[l].astype(BF16), tm_dense, te)
            xb = _ffn(xb, row(ffn2_norm[l]), ffn2_w1[l].astype(BF16), ffn2_w3[l].astype(BF16),
                      ffn2_w2[l].astype(BF16), tm_dense, tf)
        outs.append(xb)
    return jnp.stack(outs, axis=0)
```

```python
import functools
import math

import numpy as np
import jax
import jax.numpy as jnp
from jax import lax
from jax.experimental import pallas as pl
from jax.experimental.pallas import tpu as pltpu

F32 = jnp.float32
BF16 = jnp.bfloat16

N_HEADS = 8
HEAD_DIM = 64
MIX_W = N_HEADS * HEAD_DIM
N_PAIRS = N_HEADS // 2
PAIR_W = 2 * HEAD_DIM
CHUNK = 64
CONV_W = 4
LORA_W = 256
ROPE_BASE = 10000.0
NORM_EPS = 1e-6
RWKV_GN_EPS = 64e-5
NEG_BIG = -1e30
HALO = 8

Z_COLS = 8192
(C_FQ, C_FK, C_FV, C_FO, C_RR, C_RK, C_RV, C_TQ, C_TK, C_TV, C_TG,
 C_MQ, C_MK, C_MV, C_MO) = range(15)
C_RL_256 = 7680 // LORA_W
C_GATE_128 = 7936 // 128
G_FOX_F, G_ML_I, G_ML_F = 0, 8, 16

VMEM_LIMIT = 56 * 1024 * 1024


def _cparams(sem):
    return pltpu.CompilerParams(dimension_semantics=sem, vmem_limit_bytes=VMEM_LIMIT)


def _dot(a, b):
    return jnp.dot(a, b, preferred_element_type=F32)


def _dot_nt(a, b):
    return lax.dot_general(a, b, (((1,), (1,)), ((), ())), preferred_element_type=F32)


def _dot_tn(a, b):
    return lax.dot_general(a, b, (((0,), (0,)), ((), ())), preferred_element_type=F32)


def _split2(x):
    hi = x.astype(BF16)
    lo = (x - hi.astype(F32)).astype(BF16)
    return hi, lo


def _split3(x):
    hi = x.astype(BF16)
    r = x - hi.astype(F32)
    mid = r.astype(BF16)
    lo = (r - mid.astype(F32)).astype(BF16)
    return hi, mid, lo


def _dot_exact_lhs(m01, x):
    hi, mid, lo = _split3(x)
    return _dot(m01, hi) + _dot(m01, mid) + _dot(m01, lo)


def _seg_matrix(n):
    r = lax.broadcasted_iota(jnp.int32, (n, n), 0) // HEAD_DIM
    c = lax.broadcasted_iota(jnp.int32, (n, n), 1) // HEAD_DIM
    return jnp.where(r == c, 1.0, 0.0).astype(BF16)


def _seg_sum(x, seg):
    hi, lo = _split2(x)
    return _dot(hi, seg) + _dot(lo, seg)


def _sigmoid(x):
    return 1.0 / (1.0 + jnp.exp(-x))


def _silu(x):
    return x * _sigmoid(x)


def _rms_rows(x, gain):
    ms = jnp.mean(x * x, axis=-1, keepdims=True)
    return x * lax.rsqrt(ms + NORM_EPS) * gain


def _lane_is_head0(shape):
    return lax.broadcasted_iota(jnp.int32, shape, len(shape) - 1) < HEAD_DIM


def _stack_heads(x, m0):
    z = jnp.zeros_like(x)
    return jnp.concatenate([jnp.where(m0, x, z), jnp.where(m0, z, x)], axis=0)


def _unstack_rows(x_st, m0):
    return jnp.where(m0, x_st[:CHUNK], x_st[CHUNK:])


def _block_diag_mask():
    r = lax.broadcasted_iota(jnp.int32, (PAIR_W, PAIR_W), 0)
    c = lax.broadcasted_iota(jnp.int32, (PAIR_W, PAIR_W), 1)
    return (r // HEAD_DIM) == (c // HEAD_DIM), r % HEAD_DIM, c % HEAD_DIM


def _ffn_body(x_ref, g_ref, w1_ref, w3_ref, w2_ref, o_ref, h_ref):
    @pl.when(pl.program_id(1) == 0)
    def _():
        x = x_ref[...]
        h_ref[...] = _rms_rows(x, g_ref[...]).astype(BF16)
        o_ref[...] = x

    h = h_ref[...]
    a = _dot(h, w1_ref[...])
    b = _dot(h, w3_ref[...])
    g = (0.5 * _silu(a)) * b
    o_ref[...] += _dot(g.astype(BF16), w2_ref[...])


def _ffn(x, gain, w1, w3, w2, tm, tf):
    s, d = x.shape
    dff = w1.shape[1]
    return pl.pallas_call(
        _ffn_body,
        grid=(s // tm, dff // tf),
        in_specs=[
            pl.BlockSpec((tm, d), lambda i, j: (i, 0)),
            pl.BlockSpec((1, d), lambda i, j: (0, 0)),
            pl.BlockSpec((d, tf), lambda i, j: (0, j)),
            pl.BlockSpec((d, tf), lambda i, j: (0, j)),
            pl.BlockSpec((tf, d), lambda i, j: (j, 0)),
        ],
        out_specs=pl.BlockSpec((tm, d), lambda i, j: (i, 0)),
        out_shape=jax.ShapeDtypeStruct((s, d), F32),
        scratch_shapes=[pltpu.VMEM((tm, d), BF16)],
        compiler_params=_cparams(("parallel", "arbitrary")),
        name="ffn",
    )(x, gain, w1, w3, w2)


def _inproj_body(x_ref, g_ref, w_ref, o_ref, h_ref):
    @pl.when(pl.program_id(1) == 0)
    def _():
        h_ref[...] = _rms_rows(x_ref[...], g_ref[...]).astype(BF16)

    o_ref[...] = _dot(h_ref[...], w_ref[...])


def _inproj(x, gain, w, tm, tn):
    s, d = x.shape
    n = w.shape[1]
    return pl.pallas_call(
        _inproj_body,
        grid=(s // tm, n // tn),
        in_specs=[
            pl.BlockSpec((tm, d), lambda i, j: (i, 0)),
            pl.BlockSpec((1, d), lambda i, j: (0, 0)),
            pl.BlockSpec((d, tn), lambda i, j: (0, j)),
        ],
        out_specs=pl.BlockSpec((tm, tn), lambda i, j: (i, j)),
        out_shape=jax.ShapeDtypeStruct((s, n), F32),
        scratch_shapes=[pltpu.VMEM((tm, d), BF16)],
        compiler_params=_cparams(("parallel", "arbitrary")),
        name="inproj",
    )(x, gain, w)


def _merge_body(x_ref, g_ref, y0, y1, y2, y3, wg0, wg1, wg2, wg3, bg_ref, wb_ref, wo_ref,
                o_ref, h_ref):
    @pl.when(pl.program_id(1) == 0)
    def _():
        x = x_ref[...]
        h_ref[...] = _rms_rows(x, g_ref[...]).astype(BF16)
        o_ref[...] = x

    h = h_ref[...]
    merged = None
    for n, (y_ref, wg_ref) in enumerate(((y0, wg0), (y1, wg1), (y2, wg2), (y3, wg3))):
        gate = _sigmoid(_dot(h, wg_ref[...]) + bg_ref[n:n + 1, :])
        term = gate * _dot(y_ref[...], wb_ref[n])
        merged = term if merged is None else merged + term
    o_ref[...] += _dot(merged.astype(BF16), wo_ref[...])


def _merge(x, gain, ys, wg, bg, wb, wo, tm, te):
    s, d = x.shape
    nb = len(ys)
    ne = d // te
    y_specs = [pl.BlockSpec((tm, MIX_W), lambda i, j: (i, 0)) for _ in range(nb)]
    wg_specs = [pl.BlockSpec((d, te), functools.partial(lambda i, j, n: (0, n * ne + j), n=n))
                for n in range(nb)]
    return pl.pallas_call(
        _merge_body,
        grid=(s // tm, ne),
        in_specs=[
            pl.BlockSpec((tm, d), lambda i, j: (i, 0)),
            pl.BlockSpec((1, d), lambda i, j: (0, 0)),
            *y_specs, *wg_specs,
            pl.BlockSpec((nb, te), lambda i, j: (0, j)),
            pl.BlockSpec((nb, MIX_W, te), lambda i, j: (0, 0, j)),
            pl.BlockSpec((te, d), lambda i, j: (j, 0)),
        ],
        out_specs=pl.BlockSpec((tm, d), lambda i, j: (i, 0)),
        out_shape=jax.ShapeDtypeStruct((s, d), F32),
        scratch_shapes=[pltpu.VMEM((tm, d), BF16)],
        compiler_params=_cparams(("parallel", "arbitrary")),
        name="merge",
    )(x, gain, *ys, wg, wg, wg, wg, bg, wb, wo)


def _rope_body(pos_ref, inv_ref, sgn_ref, cos_ref, sin_ref):
    ang = pos_ref[...] * inv_ref[...]
    cos_ref[...] = jnp.cos(ang)
    sin_ref[...] = jnp.sin(ang) * sgn_ref[...]


def _rope_tables(pos_col, tm):
    s = pos_col.shape[0]
    half = HEAD_DIM // 2
    inv = ROPE_BASE ** (-jnp.arange(0, HEAD_DIM, 2, dtype=F32) / HEAD_DIM)
    inv = jnp.tile(inv, PAIR_W // half)[None, :]
    sgn = jnp.tile(jnp.concatenate([-jnp.ones((half,), F32), jnp.ones((half,), F32)]), 2)[None, :]
    return pl.pallas_call(
        _rope_body,
        grid=(s // tm,),
        in_specs=[pl.BlockSpec((tm, 1), lambda i: (i, 0)),
                  pl.BlockSpec((1, PAIR_W), lambda i: (0, 0)),
                  pl.BlockSpec((1, PAIR_W), lambda i: (0, 0))],
        out_specs=[pl.BlockSpec((tm, PAIR_W), lambda i: (i, 0))] * 2,
        out_shape=[jax.ShapeDtypeStruct((s, PAIR_W), F32)] * 2,
        compiler_params=_cparams(("parallel",)),
        name="rope_tables",
    )(pos_col, inv, sgn)


def _gates_body(z_ref, bias_ref, g_ref, gtc_ref, carry_ref):
    tm = z_ref.shape[0]

    @pl.when(pl.program_id(0) == 0)
    def _():
        carry_ref[...] = jnp.zeros_like(carry_ref)

    v = z_ref[...] + bias_ref[...]
    ls = jnp.minimum(v, 0.0) - jnp.log1p(jnp.exp(-jnp.abs(v)))
    r = lax.broadcasted_iota(jnp.int32, (tm, tm), 0)
    c = lax.broadcasted_iota(jnp.int32, (tm, tm), 1)
    tri = jnp.where(c <= r, 1.0, 0.0).astype(BF16)
    tri_chunk = jnp.where((c <= r) & ((r // CHUNK) == (c // CHUNK)), 1.0, 0.0).astype(BF16)
    run = _dot_exact_lhs(tri, ls) + carry_ref[...]
    loc = _dot_exact_lhs(tri_chunk, ls)
    lane = lax.broadcasted_iota(jnp.int32, (1, 128), 1)
    out = jnp.where(lane < G_ML_I, run, jnp.where(lane < G_ML_F, v, loc))
    carry_ref[...] = run[tm - 1:tm, :]
    g_ref[...] = out
    out_t = out.T
    for ci in range(tm // CHUNK):
        gtc_ref[ci] = out_t[:32, ci * CHUNK:(ci + 1) * CHUNK]


def _gates(z, bias, tm):
    s = z.shape[0]
    return pl.pallas_call(
        _gates_body,
        grid=(s // tm,),
        in_specs=[pl.BlockSpec((tm, 128), lambda i: (i, C_GATE_128)),
                  pl.BlockSpec((1, 128), lambda i: (0, 0))],
        out_specs=[pl.BlockSpec((tm, 128), lambda i: (i, 0)),
                   pl.BlockSpec((tm // CHUNK, 32, CHUNK), lambda i: (i, 0, 0))],
        out_shape=[jax.ShapeDtypeStruct((s, 128), F32),
                   jax.ShapeDtypeStruct((s // CHUNK, 32, CHUNK), F32)],
        scratch_shapes=[pltpu.VMEM((1, 128), F32)],
        compiler_params=_cparams(("arbitrary",)),
        name="gates",
    )(z, bias)


FOX_SLOT = 128
FOX_BIAS_LANE = HEAD_DIM
LOG2E = 1.4426950408889634


def _fox_prep_body(zq_ref, zk_ref, zv_ref, g_ref, qg_ref, kg_ref, spread_ref, place_ref,
                   q1_ref, k1_ref, q_ref, k_ref, v_ref):
    seg = _seg_matrix(MIX_W)
    q = zq_ref[...]
    k = zk_ref[...]
    q_ms = _seg_sum(q * q, seg) * (1.0 / HEAD_DIM)
    k_ms = _seg_sum(k * k, seg) * (1.0 / HEAD_DIM)
    qn = (q * lax.rsqrt(q_ms + NORM_EPS) * qg_ref[...] * (HEAD_DIM ** -0.5 * LOG2E)).astype(BF16)
    kn = (k * lax.rsqrt(k_ms + NORM_EPS) * kg_ref[...]).astype(BF16)
    spread = spread_ref[...]
    f_parts = _split3(g_ref[...] * LOG2E)
    f_q = sum(_dot(part, place_ref[n]) for n, part in enumerate(f_parts))
    f_k = sum(_dot(part, place_ref[3 + n]) for n, part in enumerate(f_parts))
    q_ref[...] = (_dot(qn, spread) + f_q + q1_ref[...]).astype(BF16)
    k_ref[...] = (_dot(kn, spread) - f_k + k1_ref[...]).astype(BF16)
    v_ref[...] = zv_ref[...].astype(BF16)


def _fox_layout_consts():
    spread = np.zeros((MIX_W, N_HEADS * FOX_SLOT), np.float32)
    place = np.zeros((6, 128, N_HEADS * FOX_SLOT), np.float32)
    q_ones = np.zeros((1, N_HEADS * FOX_SLOT), np.float32)
    k_ones = np.zeros((1, N_HEADS * FOX_SLOT), np.float32)
    for h in range(N_HEADS):
        for dd in range(HEAD_DIM):
            spread[h * HEAD_DIM + dd, h * FOX_SLOT + dd] = 1.0
        for n in range(3):
            place[n, G_FOX_F + h, h * FOX_SLOT + FOX_BIAS_LANE + n] = 1.0
            place[3 + n, G_FOX_F + h, h * FOX_SLOT + FOX_BIAS_LANE + 3 + n] = 1.0
            k_ones[0, h * FOX_SLOT + FOX_BIAS_LANE + n] = 1.0
            q_ones[0, h * FOX_SLOT + FOX_BIAS_LANE + 3 + n] = 1.0
    return (jnp.asarray(spread, BF16), jnp.asarray(place, BF16), jnp.asarray(q_ones), jnp.asarray(k_ones))


def _fox_prep(z, g, q_gain, k_gain, tm):
    s = z.shape[0]
    wide = N_HEADS * FOX_SLOT
    spread, place, q_ones, k_ones = _fox_layout_consts()
    zspec = lambda c: pl.BlockSpec((tm, MIX_W), lambda i: (i, c))
    return pl.pallas_call(
        _fox_prep_body,
        grid=(s // tm,),
        in_specs=[zspec(C_FQ), zspec(C_FK), zspec(C_FV),
                  pl.BlockSpec((tm, 128), lambda i: (i, 0)),
                  pl.BlockSpec((1, MIX_W), lambda i: (0, 0)),
                  pl.BlockSpec((1, MIX_W), lambda i: (0, 0)),
                  pl.BlockSpec((MIX_W, wide), lambda i: (0, 0)),
                  pl.BlockSpec((6, 128, wide), lambda i: (0, 0, 0)),
                  pl.BlockSpec((1, wide), lambda i: (0, 0)),
                  pl.BlockSpec((1, wide), lambda i: (0, 0))],
        out_specs=[pl.BlockSpec((tm, wide), lambda i: (i, 0)),
                   pl.BlockSpec((tm, wide), lambda i: (i, 0)),
                   pl.BlockSpec((tm, MIX_W), lambda i: (i, 0))],
        out_shape=[jax.ShapeDtypeStruct((s, wide), BF16),
                   jax.ShapeDtypeStruct((s, wide), BF16),
                   jax.ShapeDtypeStruct((s, MIX_W), BF16)],
        compiler_params=_cparams(("parallel",)),
        name="fox_prep",
    )(z, z, z, g, q_gain, k_gain, spread, place, q_ones, k_ones)


def _fox_body(it_ref, jt_ref, q_ref, k_ref, v_ref, og_ref, o_ref, m_ref, l_ref, acc_ref):
    t = pl.program_id(1)
    i = it_ref[t]
    j = jt_ref[t]
    tq, tk = q_ref.shape[0], k_ref.shape[0]
    m0 = _lane_is_head0((1, PAIR_W))

    @pl.when(j == 0)
    def _():
        m_ref[...] = jnp.full_like(m_ref, NEG_BIG)
        l_ref[...] = jnp.zeros_like(l_ref)
        acc_ref[...] = jnp.zeros_like(acc_ref)

    def step(on_diagonal):
        v = v_ref[...]
        pvs, alphas = [], []
        for e in range(2):
            slot = slice(e * FOX_SLOT, (e + 1) * FOX_SLOT)
            s = _dot_nt(q_ref[:, slot], k_ref[:, slot])
            if on_diagonal:
                row = lax.broadcasted_iota(jnp.int32, (tq, tk), 0)
                col = lax.broadcasted_iota(jnp.int32, (tq, tk), 1)
                s = jnp.where(col <= row, s, NEG_BIG)
            m_prev = m_ref[e]
            m_new = jnp.maximum(m_prev, jnp.max(s, axis=1, keepdims=True))
            alpha = jnp.exp2(m_prev - m_new)
            pr = jnp.exp2(s - m_new)
            l_ref[e] = alpha * l_ref[e] + jnp.sum(pr, axis=1, keepdims=True)
            m_ref[e] = m_new
            pvs.append(_dot(pr.astype(BF16), v))
            alphas.append(alpha)
        acc_ref[...] = (jnp.where(m0, alphas[0], alphas[1]) * acc_ref[...]
                        + jnp.where(m0, pvs[0], pvs[1]))

    @pl.when(j < i)
    def _():
        step(False)

    @pl.when(j == i)
    def _():
        step(True)
        l_pair = jnp.where(m0, l_ref[0], l_ref[1])
        o_ref[...] = (acc_ref[...] / l_pair * _sigmoid(og_ref[...])).astype(o_ref.dtype)


def _fox_attention(q, k, v, z, tq):
    s = v.shape[0]
    nq = s // tq
    ii, jj = np.tril_indices(nq)
    it = jnp.asarray(ii, jnp.int32)
    jt = jnp.asarray(jj, jnp.int32)
    fo_blk = C_FO * (MIX_W // PAIR_W)
    grid_spec = pltpu.PrefetchScalarGridSpec(
        num_scalar_prefetch=2,
        grid=(N_PAIRS, len(ii)),
        in_specs=[
            pl.BlockSpec((tq, 2 * FOX_SLOT), lambda p, t, it, jt: (it[t], p)),
            pl.BlockSpec((tq, 2 * FOX_SLOT), lambda p, t, it, jt: (jt[t], p)),
            pl.BlockSpec((tq, PAIR_W), lambda p, t, it, jt: (jt[t], p)),
            pl.BlockSpec((tq, PAIR_W), lambda p, t, it, jt: (it[t], fo_blk + p)),
        ],
        out_specs=pl.BlockSpec((tq, PAIR_W), lambda p, t, it, jt: (it[t], p)),
        scratch_shapes=[pltpu.VMEM((2, tq, 1), F32), pltpu.VMEM((2, tq, 1), F32),
                        pltpu.VMEM((tq, PAIR_W), F32)],
    )
    return pl.pallas_call(
        _fox_body,
        grid_spec=grid_spec,
        out_shape=jax.ShapeDtypeStruct((s, MIX_W), BF16),
        compiler_params=_cparams(("parallel", "arbitrary")),
        name="fox_attention",
    )(it, jt, q, k, v, z)


def _retention_body(zq_ref, zk_ref, zv_ref, zg_ref, cos_ref, sin_ref, intra_ref, qd_ref, kd_ref,
                    cd_ref, gain_ref, o_ref, st_ref, q_scr, k_scr, y_scr):
    tm = zq_ref.shape[0]

    @pl.when(pl.program_id(0) == 0)
    def _():
        st_ref[...] = jnp.zeros_like(st_ref)

    lane = lax.broadcasted_iota(jnp.int32, (1, PAIR_W), 1)
    m0 = lane < HEAD_DIM
    first_half = (lane & (HEAD_DIM // 2)) == 0
    cos = cos_ref[...]
    sin = sin_ref[...]
    for p in range(N_PAIRS):
        cs = slice(p * PAIR_W, (p + 1) * PAIR_W)
        for src, dst, scale in ((zq_ref, q_scr, 1.0), (zk_ref, k_scr, HEAD_DIM ** -0.5)):
            x = src[:, cs]
            swapped = jnp.where(first_half, pltpu.roll(x, PAIR_W - HEAD_DIM // 2, 1),
                                pltpu.roll(x, HEAD_DIM // 2, 1))
            dst[:, cs] = (x * cos + swapped * sin) * scale
    bd, _, _ = _block_diag_mask()

    pairs = range(N_PAIRS)
    pair_cols = [slice(p * PAIR_W, (p + 1) * PAIR_W) for p in pairs]

    def chunk(c, carry):
        rows = pl.ds(pl.multiple_of(c * CHUNK, CHUNK), CHUNK)
        q = [q_scr[rows, cols] for cols in pair_cols]
        k = [k_scr[rows, cols] for cols in pair_cols]
        vb = [zv_ref[rows, cols].astype(BF16) for cols in pair_cols]
        st = [st_ref[p] for p in pairs]
        sc = [_dot_nt(_stack_heads(q[p], m0).astype(BF16), k[p].astype(BF16)) * intra_ref[p]
              for p in pairs]
        o_inter = [_dot((q[p] * qd_ref[p]).astype(BF16), st[p].astype(BF16)) for p in pairs]
        upd = [_dot_tn((k[p] * kd_ref[p]).astype(BF16), vb[p]) for p in pairs]
        o_intra = [_unstack_rows(_dot(sc[p].astype(BF16), vb[p]), m0) for p in pairs]
        for p in pairs:
            y_scr[rows, pair_cols[p]] = o_intra[p] + o_inter[p]
            st_ref[p] = st[p] * cd_ref[p] + jnp.where(bd, upd[p], 0.0)
        return carry

    lax.fori_loop(0, tm // CHUNK, chunk, 0)
    y = y_scr[...]
    ms = _seg_sum(y * y, _seg_matrix(MIX_W)) * (1.0 / HEAD_DIM)
    o_ref[...] = (y * lax.rsqrt(ms + NORM_EPS) * gain_ref[...] * _silu(zg_ref[...])).astype(o_ref.dtype)


def _retention_tables():
    hh = jnp.arange(N_HEADS, dtype=F32)
    log_gamma = jnp.log1p(-jnp.exp2(-5.0 - hh))
    idx = jnp.arange(CHUNK, dtype=F32)
    intra = jnp.exp(log_gamma[:, None, None] * jnp.abs(idx[:, None] - idx[None, :]))
    q_decay = jnp.exp(log_gamma[:, None] * (idx + 1.0))
    k_decay = jnp.exp(log_gamma[:, None] * (CHUNK - 1.0 - idx))
    chunk_decay = jnp.exp(log_gamma * CHUNK)
    intra_st = intra.reshape(N_PAIRS, 2 * CHUNK, CHUNK)
    to_pair = lambda t: jnp.repeat(t.reshape(N_PAIRS, 2, CHUNK).transpose(0, 2, 1), HEAD_DIM, axis=2)
    cd = jnp.repeat(chunk_decay.reshape(N_PAIRS, 1, 2), HEAD_DIM, axis=2)
    return intra_st, to_pair(q_decay), to_pair(k_decay), cd


def _retention(z, cos, sin, gain, tm):
    s = z.shape[0]
    intra_st, qd, kd, cd = _retention_tables()
    zspec = lambda c: pl.BlockSpec((tm, MIX_W), lambda i: (i, c))
    full = lambda a: pl.BlockSpec(a.shape, lambda i: (0,) * a.ndim)
    return pl.pallas_call(
        _retention_body,
        grid=(s // tm,),
        in_specs=[zspec(C_TQ), zspec(C_TK), zspec(C_TV), zspec(C_TG),
                  pl.BlockSpec((tm, PAIR_W), lambda i: (i, 0)),
                  pl.BlockSpec((tm, PAIR_W), lambda i: (i, 0)),
                  full(intra_st), full(qd), full(kd), full(cd),
                  pl.BlockSpec((1, MIX_W), lambda i: (0, 0))],
        out_specs=pl.BlockSpec((tm, MIX_W), lambda i: (i, 0)),
        out_shape=jax.ShapeDtypeStruct((s, MIX_W), BF16),
        scratch_shapes=[pltpu.VMEM((N_PAIRS, PAIR_W, PAIR_W), F32),
                        pltpu.VMEM((tm, MIX_W), F32), pltpu.VMEM((tm, MIX_W), F32),
                        pltpu.VMEM((tm, MIX_W), F32)],
        compiler_params=_cparams(("arbitrary",)),
        name="retention",
    )(z, z, z, z, cos, sin, intra_st, qd, kd, cd, gain)


def _load_with_halo(ext_ref, cur_ref, halo_ref, first):
    halo = halo_ref[...]
    ext_ref[0:HALO, :] = jnp.where(first, jnp.zeros_like(halo), halo)
    ext_ref[HALO:, :] = cur_ref[...]


def _mlstm_body(zq_ref, zk_ref, hq_ref, hk_ref, zv_ref, zo_ref, g_ref, gtc_ref, cw_ref, cb_ref,
                gain_ref, o_ref, c_ref, n_ref, m_ref, ext_scr, q_scr, k_scr, y_scr):
    tm = zq_ref.shape[0]
    first = pl.program_id(0) == 0

    @pl.when(first)
    def _():
        c_ref[...] = jnp.zeros_like(c_ref)
        n_ref[...] = jnp.zeros_like(n_ref)
        m_ref[...] = jnp.zeros_like(m_ref)

    for part, (cur, halo, dst, scale) in enumerate(((zq_ref, hq_ref, q_scr, 1.0),
                                                    (zk_ref, hk_ref, k_scr, HEAD_DIM ** -0.5))):
        _load_with_halo(ext_scr, cur, halo, first)
        cols = slice(part * MIX_W, (part + 1) * MIX_W)
        acc = cb_ref[:, cols]
        for tap in range(CONV_W):
            off = HALO - (CONV_W - 1) + tap
            acc = acc + cw_ref[tap:tap + 1, cols] * ext_scr[off:off + tm, :]
        dst[...] = _silu(acc) * scale

    lane = lax.broadcasted_iota(jnp.int32, (1, PAIR_W), 1)
    m0 = lane < HEAD_DIM
    bd, _, _ = _block_diag_mask()
    ri = lax.broadcasted_iota(jnp.int32, (CHUNK, CHUNK), 0)
    ci = lax.broadcasted_iota(jnp.int32, (CHUNK, CHUNK), 1)
    causal = ci <= ri

    pairs = range(N_PAIRS)
    pair_cols = [slice(p * PAIR_W, (p + 1) * PAIR_W) for p in pairs]
    pair = lambda t0, t1: jnp.where(m0, t0, t1)

    def head_gates(g, gt, m_pair, h, e):
        b_c = g[:, G_ML_F + h:G_ML_F + h + 1]
        li_c = g[:, G_ML_I + h:G_ML_I + h + 1]
        b_r = gt[G_ML_F + h:G_ML_F + h + 1, :]
        li_r = gt[G_ML_I + h:G_ML_I + h + 1, :]
        m_prev = m_pair[:, e * HEAD_DIM:e * HEAD_DIM + 1]
        log_d = jnp.where(causal, b_c - b_r + li_r, NEG_BIG)
        log_inter = b_c + m_prev
        m_t = jnp.maximum(log_inter, jnp.max(log_d, axis=1, keepdims=True))
        b_last = b_c[CHUNK - 1:CHUNK, :]
        log_w = b_last - b_c + li_c
        m_new = jnp.maximum(b_last + m_prev, jnp.max(log_w, axis=0, keepdims=True))
        return dict(d_mat=jnp.exp(log_d - m_t), inter=jnp.exp(log_inter - m_t), floor=jnp.exp(-m_t),
                    w=jnp.exp(log_w - m_new), dec=jnp.exp(b_last + m_prev - m_new), m_new=m_new)

    def chunk(c, carry):
        rows = pl.ds(pl.multiple_of(c * CHUNK, CHUNK), CHUNK)
        g = g_ref[rows, :]
        gt = gtc_ref[c]
        q = [q_scr[rows, cols] for cols in pair_cols]
        k = [k_scr[rows, cols] for cols in pair_cols]
        vb = [zv_ref[rows, cols].astype(BF16) for cols in pair_cols]
        cst = [c_ref[p] for p in pairs]
        nst = [n_ref[p] for p in pairs]
        hg = [[head_gates(g, gt, m_ref[p], 2 * p + e, e) for e in range(2)] for p in pairs]
        sc = [_dot_nt(_stack_heads(q[p], m0).astype(BF16), k[p].astype(BF16)) for p in pairs]
        q_c = [_dot(q[p].astype(BF16), cst[p].astype(BF16)) for p in pairs]
        sc = [sc[p] * jnp.concatenate([hg[p][0]["d_mat"], hg[p][1]["d_mat"]], axis=0) for p in pairs]
        num = [_unstack_rows(_dot(sc[p].astype(BF16), vb[p]), m0) for p in pairs]
        den_intra = [jnp.sum(sc[p], axis=1, keepdims=True) for p in pairs]
        kw = [k[p] * pair(hg[p][0]["w"], hg[p][1]["w"]) for p in pairs]
        upd = [_dot_tn(kw[p].astype(BF16), vb[p]) for p in pairs]
        for p in pairs:
            h0, h1 = hg[p]
            num_p = num[p] + pair(h0["inter"], h1["inter"]) * q_c[p]
            qn = q[p] * nst[p]
            zqn = jnp.zeros_like(qn)
            qn0 = jnp.sum(jnp.where(m0, qn, zqn), axis=1, keepdims=True)
            qn1 = jnp.sum(jnp.where(m0, zqn, qn), axis=1, keepdims=True)
            den0 = den_intra[p][:CHUNK] + h0["inter"] * qn0
            den1 = den_intra[p][CHUNK:] + h1["inter"] * qn1
            denom = pair(jnp.maximum(jnp.abs(den0), h0["floor"]), jnp.maximum(jnp.abs(den1), h1["floor"]))
            y_scr[rows, pair_cols[p]] = num_p / denom
            dec_pair = pair(h0["dec"], h1["dec"])
            c_ref[p] = dec_pair * cst[p] + jnp.where(bd, upd[p], 0.0)
            n_ref[p] = dec_pair * nst[p] + jnp.sum(kw[p], axis=0, keepdims=True)
            m_ref[p] = pair(h0["m_new"], h1["m_new"])
        return carry

    lax.fori_loop(0, tm // CHUNK, chunk, 0)
    y = y_scr[...] * _sigmoid(zo_ref[...])
    ms = _seg_sum(y * y, _seg_matrix(MIX_W)) * (1.0 / HEAD_DIM)
    o_ref[...] = (y * lax.rsqrt(ms + NORM_EPS) * gain_ref[...]).astype(o_ref.dtype)


def _halo_spec(tm, width, c):
    step = tm // HALO
    return pl.BlockSpec((HALO, width), lambda i: (jnp.maximum(i * step - 1, 0), c))


def _mlstm(z, g, gtc, conv_w, conv_b, gain, tm):
    s = z.shape[0]
    zspec = lambda c: pl.BlockSpec((tm, MIX_W), lambda i: (i, c))
    return pl.pallas_call(
        _mlstm_body,
        grid=(s // tm,),
        in_specs=[zspec(C_MQ), zspec(C_MK), _halo_spec(tm, MIX_W, C_MQ), _halo_spec(tm, MIX_W, C_MK),
                  zspec(C_MV), zspec(C_MO),
                  pl.BlockSpec((tm, 128), lambda i: (i, 0)),
                  pl.BlockSpec((tm // CHUNK, 32, CHUNK), lambda i: (i, 0, 0)),
                  pl.BlockSpec((CONV_W, 2 * MIX_W), lambda i: (0, 0)),
                  pl.BlockSpec((1, 2 * MIX_W), lambda i: (0, 0)),
                  pl.BlockSpec((1, MIX_W), lambda i: (0, 0))],
        out_specs=pl.BlockSpec((tm, MIX_W), lambda i: (i, 0)),
        out_shape=jax.ShapeDtypeStruct((s, MIX_W), BF16),
        scratch_shapes=[pltpu.VMEM((N_PAIRS, PAIR_W, PAIR_W), F32),
                        pltpu.VMEM((N_PAIRS, 1, PAIR_W), F32),
                        pltpu.VMEM((N_PAIRS, 1, PAIR_W), F32),
                        pltpu.VMEM((tm + HALO, MIX_W), F32),
                        pltpu.VMEM((tm, MIX_W), F32), pltpu.VMEM((tm, MIX_W), F32),
                        pltpu.VMEM((tm, MIX_W), F32)],
        compiler_params=_cparams(("arbitrary",)),
        name="mlstm",
    )(z, z, z, z, z, z, g, gtc, conv_w, conv_b, gain)


def _rwkv_body(zr_ref, zk_ref, zv_ref, zl_ref, hr_ref, hk_ref, hv_ref, hl_ref, mu_ref, mul_ref,
               wl_ref, w0_ref, a0_ref, kk_ref, ka_ref, rk_ref, gnw_ref, gnb_ref, o_ref,
               h_ref, ext_scr, extl_scr, rt_scr, kt_scr, kb_scr, bb_scr, kh_scr, bh_scr,
               gam_scr, v_scr, y_scr):
    tm = zr_ref.shape[0]
    first = pl.program_id(0) == 0

    @pl.when(first)
    def _():
        h_ref[...] = jnp.zeros_like(h_ref)

    def shifted(ext, cur_ref, halo_ref, mu):
        _load_with_halo(ext, cur_ref, halo_ref, first)
        cur = ext[HALO:HALO + tm, :]
        prev = ext[HALO - 1:HALO - 1 + tm, :]
        return cur + mu * (prev - cur)

    r = shifted(ext_scr, zr_ref, hr_ref, mu_ref[:, 0:MIX_W])
    k = shifted(ext_scr, zk_ref, hk_ref, mu_ref[:, MIX_W:2 * MIX_W])
    v = shifted(ext_scr, zv_ref, hv_ref, mu_ref[:, 2 * MIX_W:3 * MIX_W])
    lora_in = shifted(extl_scr, zl_ref, hl_ref, mul_ref[...])
    lane_l = lax.broadcasted_iota(jnp.int32, (1, LORA_W), 1)
    act = jnp.where(lane_l < 64, jnp.tanh(lora_in), jnp.where(lane_l < 128, lora_in, _sigmoid(lora_in)))
    lora = _dot(act.astype(BF16), wl_ref[...])
    logw = -_sigmoid(w0_ref[...] + lora[:, 0:MIX_W]) * math.exp(-0.5)
    a = _sigmoid(a0_ref[...] + lora[:, MIX_W:2 * MIX_W])
    gate = lora[:, 2 * MIX_W:3 * MIX_W]

    seg = _seg_matrix(MIX_W)
    kk = k * kk_ref[...]
    kk = kk * lax.rsqrt(jnp.maximum(_seg_sum(kk * kk, seg), 1e-12))
    k2 = k * (1.0 + (a - 1.0) * ka_ref[...])
    b = kk * a
    bonus = _seg_sum(r * k2 * rk_ref[...], seg) * v

    ri = lax.broadcasted_iota(jnp.int32, (tm, tm), 0)
    ci = lax.broadcasted_iota(jnp.int32, (tm, tm), 1)
    same = (ri // CHUNK) == (ci // CHUNK)
    lower = jnp.where(same & (ci <= ri), 1.0, 0.0).astype(BF16)
    upper = jnp.where(same & (ci > ri), 1.0, 0.0).astype(BF16)
    cl = _dot_exact_lhs(lower, logw)
    cs = _dot_exact_lhs(upper, logw)
    e_cl = jnp.exp(cl)
    e_ncl = jnp.exp(-cl)
    e_cs = jnp.exp(cs)
    rt_scr[...] = r * e_cl
    kt_scr[...] = kk * jnp.exp(cl - logw)
    kb_scr[...] = k2 * e_ncl
    bb_scr[...] = b * e_ncl
    kh_scr[...] = k2 * e_cs
    bh_scr[...] = b * e_cs
    gam_scr[...] = jnp.exp(cl + cs)
    v_scr[...] = v

    lane = lax.broadcasted_iota(jnp.int32, (1, PAIR_W), 1)
    m0 = lane < HEAD_DIM
    bd, rr, cc = _block_diag_mask()
    strict = bd & (cc < rr)
    incl = bd & (cc <= rr)
    eye = bd & (cc == rr)
    eye_f = jnp.where(eye, 1.0, 0.0)

    pairs = range(N_PAIRS)
    pair_cols = [slice(p * PAIR_W, (p + 1) * PAIR_W) for p in pairs]

    def chunk(c, carry):
        rows = pl.ds(pl.multiple_of(c * CHUNK, CHUNK), CHUNK)
        first8 = pl.ds(pl.multiple_of(c * CHUNK, CHUNK), 8)
        stacked = lambda scr: [_stack_heads(scr[rows, cols], m0) for cols in pair_cols]
        rt_st = stacked(rt_scr)
        kt_st = stacked(kt_scr)
        v_st = [t.astype(BF16) for t in stacked(v_scr)]
        kh_st = [t.astype(BF16) for t in stacked(kh_scr)]
        bh_st = [t.astype(BF16) for t in stacked(bh_scr)]
        twice = lambda scr: [jnp.concatenate([scr[rows, cols].astype(BF16)] * 2, axis=0)
                             for cols in pair_cols]
        bb2 = twice(bb_scr)
        kb2 = twice(kb_scr)
        gam = [gam_scr[first8, cols][0:1] for cols in pair_cols]
        lhs = [jnp.concatenate([kt_st[p], rt_st[p]], axis=0).astype(BF16) for p in pairs]
        g_b = [_dot_nt(lhs[p], bb2[p]) for p in pairs]
        g_k = [_dot_nt(lhs[p], kb2[p]) for p in pairs]
        x = [jnp.where(strict, -g_b[p][:PAIR_W], 0.0) for p in pairs]
        q_bd = [jnp.where(incl, g_b[p][PAIR_W:], 0.0).astype(BF16) for p in pairs]
        b_bd = [jnp.where(strict, g_k[p][:PAIR_W], 0.0).astype(BF16) for p in pairs]
        p_bd = [jnp.where(incl, g_k[p][PAIR_W:], 0.0).astype(BF16) for p in pairs]
        bv = [_dot(b_bd[p], v_st[p]).astype(BF16) for p in pairs]
        pv = [_dot(p_bd[p], v_st[p]) for p in pairs]
        khv = [_dot_tn(kh_st[p], v_st[p]) for p in pairs]
        t_inv = [eye_f + x[p] for p in pairs]
        for _ in range(5):
            xb = [x[p].astype(BF16) for p in pairs]
            x = [_dot(xb[p], xb[p]) for p in pairs]
            t_inv = [t_inv[p] + _dot(t_inv[p].astype(BF16), x[p].astype(BF16)) for p in pairs]
        tb = [t_inv[p].astype(BF16) for p in pairs]
        kt2 = [_dot(tb[p], kt_st[p].astype(BF16)).astype(BF16) for p in pairs]
        w1 = [_dot(tb[p], bv[p]).astype(BF16) for p in pairs]
        y1 = [(rt_st[p] - _dot(q_bd[p], kt2[p])).astype(BF16) for p in pairs]
        y0 = [pv[p] - _dot(q_bd[p], w1[p]) for p in pairs]
        m_mat = [(jnp.where(eye, gam[p], 0.0) - _dot_tn(bh_st[p], kt2[p])).astype(BF16) for p in pairs]
        n_mat = [khv[p] - _dot_tn(bh_st[p], w1[p]) for p in pairs]
        hbb = [h_ref[p].astype(BF16) for p in pairs]
        y_st = [_dot(y1[p], hbb[p]) + y0[p] for p in pairs]
        h_new = [_dot(m_mat[p], hbb[p]) + n_mat[p] for p in pairs]
        for p in pairs:
            y_scr[rows, pair_cols[p]] = y_st[p][:CHUNK] + y_st[p][CHUNK:]
            h_ref[p] = h_new[p]
        return carry

    lax.fori_loop(0, tm // CHUNK, chunk, 0)

    y = y_scr[...]
    mean = _seg_sum(y, seg) * (1.0 / HEAD_DIM)
    yc = y - mean
    var = _seg_sum(yc * yc, seg) * (1.0 / HEAD_DIM)
    yn = yc * lax.rsqrt(var + RWKV_GN_EPS) * gnw_ref[...] + gnb_ref[...]
    o_ref[...] = ((yn + bonus) * gate).astype(o_ref.dtype)


def _rwkv(z, mu, mu_l, w_lora, w0, a0, k_k, k_a, r_k, gn_w, gn_b, tm):
    s = z.shape[0]
    zspec = lambda c: pl.BlockSpec((tm, MIX_W), lambda i: (i, c))
    row = lambda w: pl.BlockSpec((1, w), lambda i: (0, 0))
    big = pltpu.VMEM((tm, MIX_W), F32)
    return pl.pallas_call(
        _rwkv_body,
        grid=(s // tm,),
        in_specs=[zspec(C_RR), zspec(C_RK), zspec(C_RV),
                  pl.BlockSpec((tm, LORA_W), lambda i: (i, C_RL_256)),
                  _halo_spec(tm, MIX_W, C_RR), _halo_spec(tm, MIX_W, C_RK), _halo_spec(tm, MIX_W, C_RV),
                  _halo_spec(tm, LORA_W, C_RL_256),
                  row(3 * MIX_W), row(LORA_W),
                  pl.BlockSpec((LORA_W, 3 * MIX_W), lambda i: (0, 0)),
                  row(MIX_W), row(MIX_W), row(MIX_W), row(MIX_W), row(MIX_W), row(MIX_W), row(MIX_W)],
        out_specs=pl.BlockSpec((tm, MIX_W), lambda i: (i, 0)),
        out_shape=jax.ShapeDtypeStruct((s, MIX_W), BF16),
        scratch_shapes=[pltpu.VMEM((N_PAIRS, PAIR_W, PAIR_W), F32),
                        pltpu.VMEM((tm + HALO, MIX_W), F32), pltpu.VMEM((tm + HALO, LORA_W), F32),
                        big, big, big, big, big, big, big, big, big],
        compiler_params=_cparams(("arbitrary",)),
        name="rwkv7",
    )(z, z, z, z, z, z, z, z, mu, mu_l, w_lora, w0, a0, k_k, k_a, r_k, gn_w, gn_b)


def _w_in_column_map():
    fox, rwkv, ret, ml = 0, 2056, 3848, 5896
    src = np.full((Z_COLS,), -1, np.int64)

    def put(dst, start, n):
        src[dst:dst + n] = np.arange(start, start + n)

    for blk, start in ((C_FQ, fox), (C_FK, fox + 512), (C_FV, fox + 1024), (C_FO, fox + 1536),
                       (C_RR, rwkv), (C_RK, rwkv + 512), (C_RV, rwkv + 1024),
                       (C_TQ, ret), (C_TK, ret + 512), (C_TV, ret + 1024), (C_TG, ret + 1536),
                       (C_MQ, ml), (C_MK, ml + 512), (C_MV, ml + 1024), (C_MO, ml + 1536)):
        put(blk * MIX_W, start, MIX_W)
    put(C_RL_256 * LORA_W, rwkv + 1536, LORA_W)
    gate0 = C_GATE_128 * 128
    put(gate0 + G_FOX_F, fox + 2048, N_HEADS)
    put(gate0 + G_ML_I, ml + 2048, N_HEADS)
    put(gate0 + G_ML_F, ml + 2056, N_HEADS)
    return src


def _permute_w_in(w_in):
    src = _w_in_column_map()
    parts, start = [], 0
    for pos in range(1, Z_COLS + 1):
        run_continues = pos < Z_COLS and (
            (src[pos] < 0 and src[pos - 1] < 0) or (src[pos - 1] >= 0 and src[pos] == src[pos - 1] + 1))
        if not run_continues:
            n = pos - start
            if src[start] < 0:
                parts.append(jnp.zeros((w_in.shape[0], n), BF16))
            else:
                parts.append(w_in[:, src[start]:src[start] + n].astype(BF16))
            start = pos
    return jnp.concatenate(parts, axis=1)


def kernel(x, positions, ffn1_norm, ffn1_w1, ffn1_w3, ffn1_w2, mix_norm, w_in, fox_f_bias, fox_q_gain, fox_k_gain, rwkv_shift_mu, rwkv_w0, rwkv_w2, rwkv_a0, rwkv_a2, rwkv_g2, rwkv_k_k, rwkv_k_a, rwkv_r_k, rwkv_gn_w, rwkv_gn_b, ret_gn_gain, mlstm_conv_w, mlstm_conv_b, mlstm_i_bias, mlstm_f_bias, mlstm_gn_gain, w_merge_gate, merge_gate_bias, w_branch, w_out, ffn2_norm, ffn2_w1, ffn2_w3, ffn2_w2):
    batch, seq, d = x.shape
    depth = w_in.shape[0]
    dff = ffn1_w1.shape[-1]
    pick = lambda full, want: want if full % want == 0 else full
    tm_dense = pick(seq, 512)
    tf = pick(dff, 512)
    te = pick(d, 256)
    tn = pick(Z_COLS, 1024)
    tm_row = pick(seq, 512)
    tm_seq = pick(seq, 256)
    tq = pick(seq, 512)
    row = lambda t: t.reshape(1, -1).astype(F32)

    outs = []
    for bi in range(batch):
        xb = x[bi]
        pos_col = positions[bi].astype(F32).reshape(seq, 1)
        cos, sin = _rope_tables(pos_col, tm_row)
        for l in range(depth):
            xb = _ffn(xb, row(ffn1_norm[l]), ffn1_w1[l].astype(BF16), ffn1_w3[l].astype(BF16),
                      ffn1_w2[l].astype(BF16), tm_dense, tf)
            z = _inproj(xb, row(mix_norm[l]), _permute_w_in(w_in[l]), tm_dense, tn)

            gate_bias = jnp.zeros((128,), F32)
            gate_bias = gate_bias.at[G_FOX_F:G_FOX_F + N_HEADS].set(fox_f_bias[l])
            gate_bias = gate_bias.at[G_ML_I:G_ML_I + N_HEADS].set(mlstm_i_bias[l])
            gate_bias = gate_bias.at[G_ML_F:G_ML_F + N_HEADS].set(mlstm_f_bias[l])
            g, gtc = _gates(z, gate_bias[None, :], tm_row)

            fq, fk, fv = _fox_prep(z, g, row(jnp.tile(fox_q_gain[l], N_HEADS)),
                                   row(jnp.tile(fox_k_gain[l], N_HEADS)), tm_row)
            y_fox = _fox_attention(fq, fk, fv, z, tq)

            mu = rwkv_shift_mu[l]
            w_lora = jnp.zeros((LORA_W, 3 * MIX_W), F32)
            w_lora = w_lora.at[0:64, 0:MIX_W].set(rwkv_w2[l])
            w_lora = w_lora.at[64:128, MIX_W:2 * MIX_W].set(rwkv_a2[l])
            w_lora = w_lora.at[128:256, 2 * MIX_W:3 * MIX_W].set(rwkv_g2[l])
            y_rwkv = _rwkv(z, row(mu[:3 * MIX_W]), row(mu[3 * MIX_W:]), w_lora.astype(BF16),
                           row(rwkv_w0[l]), row(rwkv_a0[l]), row(rwkv_k_k[l]), row(rwkv_k_a[l]),
                           row(rwkv_r_k[l]), row(rwkv_gn_w[l]), row(rwkv_gn_b[l]), tm_seq)

            y_ret = _retention(z, cos, sin, row(ret_gn_gain[l]), tm_seq)

            y_m = _mlstm(z, g, gtc, mlstm_conv_w[l].astype(F32), row(mlstm_conv_b[l]),
                         row(mlstm_gn_gain[l]), tm_seq)

            xb = _merge(xb, row(mix_norm[l]), (y_fox, y_rwkv, y_ret, y_m),
                        w_merge_gate[l].reshape(d, -1).astype(BF16), merge_gate_bias[l].astype(F32),
                        w_branch[l].astype(BF16), w_out[l].astype(BF16), tm_dense, te)
            xb = _ffn(xb, row(ffn2_norm[l]), ffn2_w1[l].astype(BF16), ffn2_w3[l].astype(BF16),
                      ffn2_w2[l].astype(BF16), tm_dense, tf)
        outs.append(xb)
    return jnp.stack(outs, axis=0)
```

```python
import functools
import math

import numpy as np
import jax
import jax.numpy as jnp
from jax import lax
from jax.experimental import pallas as pl
from jax.experimental.pallas import tpu as pltpu

F32 = jnp.float32
BF16 = jnp.bfloat16

N_HEADS = 8
HEAD_DIM = 64
MIX_W = N_HEADS * HEAD_DIM
N_PAIRS = N_HEADS // 2
PAIR_W = 2 * HEAD_DIM
CHUNK = 64
CONV_W = 4
LORA_W = 256
ROPE_BASE = 10000.0
NORM_EPS = 1e-6
RWKV_GN_EPS = 64e-5
NEG_BIG = -1e30
HALO = 8

Z_COLS = 8192
(C_FQ, C_FK, C_FV, C_FO, C_RR, C_RK, C_RV, C_TQ, C_TK, C_TV, C_TG,
 C_MQ, C_MK, C_MV, C_MO) = range(15)
C_RL_256 = 7680 // LORA_W
C_GATE_128 = 7936 // 128
G_FOX_F, G_ML_I, G_ML_F = 0, 8, 16

VMEM_LIMIT = 56 * 1024 * 1024


def _cparams(sem):
    return pltpu.CompilerParams(dimension_semantics=sem, vmem_limit_bytes=VMEM_LIMIT)


def _dot(a, b):
    return jnp.dot(a, b, preferred_element_type=F32)


def _dot_nt(a, b):
    return lax.dot_general(a, b, (((1,), (1,)), ((), ())), preferred_element_type=F32)


def _dot_tn(a, b):
    return lax.dot_general(a, b, (((0,), (0,)), ((), ())), preferred_element_type=F32)


def _split2(x):
    hi = x.astype(BF16)
    lo = (x - hi.astype(F32)).astype(BF16)
    return hi, lo


def _split3(x):
    hi = x.astype(BF16)
    r = x - hi.astype(F32)
    mid = r.astype(BF16)
    lo = (r - mid.astype(F32)).astype(BF16)
    return hi, mid, lo


def _dot_exact_lhs(m01, x):
    hi, mid, lo = _split3(x)
    return _dot(m01, hi) + _dot(m01, mid) + _dot(m01, lo)


def _seg_matrix(n):
    r = lax.broadcasted_iota(jnp.int32, (n, n), 0) // HEAD_DIM
    c = lax.broadcasted_iota(jnp.int32, (n, n), 1) // HEAD_DIM
    return jnp.where(r == c, 1.0, 0.0).astype(BF16)


def _seg_sum(x, seg):
    hi, lo = _split2(x)
    return _dot(hi, seg) + _dot(lo, seg)


def _sigmoid(x):
    return 1.0 / (1.0 + jnp.exp(-x))


def _silu(x):
    return x * _sigmoid(x)


def _rms_rows(x, gain):
    ms = jnp.mean(x * x, axis=-1, keepdims=True)
    return x * lax.rsqrt(ms + NORM_EPS) * gain


def _lane_is_head0(shape):
    return lax.broadcasted_iota(jnp.int32, shape, len(shape) - 1) < HEAD_DIM


def _stack_heads(x, m0):
    z = jnp.zeros_like(x)
    return jnp.concatenate([jnp.where(m0, x, z), jnp.where(m0, z, x)], axis=0)


def _unstack_rows(x_st, m0):
    return jnp.where(m0, x_st[:CHUNK], x_st[CHUNK:])


def _block_diag_mask():
    r = lax.broadcasted_iota(jnp.int32, (PAIR_W, PAIR_W), 0)
    c = lax.broadcasted_iota(jnp.int32, (PAIR_W, PAIR_W), 1)
    return (r // HEAD_DIM) == (c // HEAD_DIM), r % HEAD_DIM, c % HEAD_DIM


def _ffn_body(x_ref, g_ref, w1_ref, w3_ref, w2_ref, o_ref, h_ref):
    @pl.when(pl.program_id(1) == 0)
    def _():
        x = x_ref[...]
        h_ref[...] = _rms_rows(x, g_ref[...]).astype(BF16)
        o_ref[...] = x

    h = h_ref[...]
    half = w1_ref.shape[1] // 2
    parts = []
    for c in range(2):
        cols = slice(c * half, (c + 1) * half)
        a = _dot(h, w1_ref[:, cols])
        b = _dot(h, w3_ref[:, cols])
        g = (0.5 * _silu(a)) * b
        parts.append(_dot(g.astype(BF16), w2_ref[cols, :]))
    o_ref[...] += parts[0] + parts[1]


def _ffn(x, gain, w1, w3, w2, tm, tf):
    s, d = x.shape
    dff = w1.shape[1]
    return pl.pallas_call(
        _ffn_body,
        grid=(s // tm, dff // tf),
        in_specs=[
            pl.BlockSpec((tm, d), lambda i, j: (i, 0)),
            pl.BlockSpec((1, d), lambda i, j: (0, 0)),
            pl.BlockSpec((d, tf), lambda i, j: (0, j)),
            pl.BlockSpec((d, tf), lambda i, j: (0, j)),
            pl.BlockSpec((tf, d), lambda i, j: (j, 0)),
        ],
        out_specs=pl.BlockSpec((tm, d), lambda i, j: (i, 0)),
        out_shape=jax.ShapeDtypeStruct((s, d), F32),
        scratch_shapes=[pltpu.VMEM((tm, d), BF16)],
        compiler_params=_cparams(("parallel", "arbitrary")),
        name="ffn",
    )(x, gain, w1, w3, w2)


def _inproj_body(x_ref, g_ref, w_ref, o_ref, h_ref):
    @pl.when(pl.program_id(1) == 0)
    def _():
        h_ref[...] = _rms_rows(x_ref[...], g_ref[...]).astype(BF16)

    o_ref[...] = _dot(h_ref[...], w_ref[...])


def _inproj(x, gain, w, tm, tn):
    s, d = x.shape
    n = w.shape[1]
    return pl.pallas_call(
        _inproj_body,
        grid=(s // tm, n // tn),
        in_specs=[
            pl.BlockSpec((tm, d), lambda i, j: (i, 0)),
            pl.BlockSpec((1, d), lambda i, j: (0, 0)),
            pl.BlockSpec((d, tn), lambda i, j: (0, j)),
        ],
        out_specs=pl.BlockSpec((tm, tn), lambda i, j: (i, j)),
        out_shape=jax.ShapeDtypeStruct((s, n), F32),
        scratch_shapes=[pltpu.VMEM((tm, d), BF16)],
        compiler_params=_cparams(("parallel", "arbitrary")),
        name="inproj",
    )(x, gain, w)


def _merge_body(x_ref, g_ref, y0, y1, y2, y3, wg0, wg1, wg2, wg3, bg_ref, wb_ref, wo_ref,
                o_ref, h_ref):
    @pl.when(pl.program_id(1) == 0)
    def _():
        x = x_ref[...]
        h_ref[...] = _rms_rows(x, g_ref[...]).astype(BF16)
        o_ref[...] = x

    h = h_ref[...]
    merged = None
    for n, (y_ref, wg_ref) in enumerate(((y0, wg0), (y1, wg1), (y2, wg2), (y3, wg3))):
        gate = _sigmoid(_dot(h, wg_ref[...]) + bg_ref[n:n + 1, :])
        term = gate * _dot(y_ref[...], wb_ref[n])
        merged = term if merged is None else merged + term
    o_ref[...] += _dot(merged.astype(BF16), wo_ref[...])


def _merge(x, gain, ys, wg, bg, wb, wo, tm, te):
    s, d = x.shape
    nb = len(ys)
    ne = d // te
    y_specs = [pl.BlockSpec((tm, MIX_W), lambda i, j: (i, 0)) for _ in range(nb)]
    wg_specs = [pl.BlockSpec((d, te), functools.partial(lambda i, j, n: (0, n * ne + j), n=n))
                for n in range(nb)]
    return pl.pallas_call(
        _merge_body,
        grid=(s // tm, ne),
        in_specs=[
            pl.BlockSpec((tm, d), lambda i, j: (i, 0)),
            pl.BlockSpec((1, d), lambda i, j: (0, 0)),
            *y_specs, *wg_specs,
            pl.BlockSpec((nb, te), lambda i, j: (0, j)),
            pl.BlockSpec((nb, MIX_W, te), lambda i, j: (0, 0, j)),
            pl.BlockSpec((te, d), lambda i, j: (j, 0)),
        ],
        out_specs=pl.BlockSpec((tm, d), lambda i, j: (i, 0)),
        out_shape=jax.ShapeDtypeStruct((s, d), F32),
        scratch_shapes=[pltpu.VMEM((tm, d), BF16)],
        compiler_params=_cparams(("parallel", "arbitrary")),
        name="merge",
    )(x, gain, *ys, wg, wg, wg, wg, bg, wb, wo)


def _rope_body(pos_ref, inv_ref, sgn_ref, cos_ref, sin_ref):
    ang = pos_ref[...] * inv_ref[...]
    cos_ref[...] = jnp.cos(ang)
    sin_ref[...] = jnp.sin(ang) * sgn_ref[...]


def _rope_tables(pos_col, tm):
    s = pos_col.shape[0]
    half = HEAD_DIM // 2
    inv = ROPE_BASE ** (-jnp.arange(0, HEAD_DIM, 2, dtype=F32) / HEAD_DIM)
    inv = jnp.tile(inv, PAIR_W // half)[None, :]
    sgn = jnp.tile(jnp.concatenate([-jnp.ones((half,), F32), jnp.ones((half,), F32)]), 2)[None, :]
    return pl.pallas_call(
        _rope_body,
        grid=(s // tm,),
        in_specs=[pl.BlockSpec((tm, 1), lambda i: (i, 0)),
                  pl.BlockSpec((1, PAIR_W), lambda i: (0, 0)),
                  pl.BlockSpec((1, PAIR_W), lambda i: (0, 0))],
        out_specs=[pl.BlockSpec((tm, PAIR_W), lambda i: (i, 0))] * 2,
        out_shape=[jax.ShapeDtypeStruct((s, PAIR_W), F32)] * 2,
        compiler_params=_cparams(("parallel",)),
        name="rope_tables",
    )(pos_col, inv, sgn)


def _gates_body(z_ref, bias_ref, g_ref, gtc_ref, carry_ref):
    tm = z_ref.shape[0]

    @pl.when(pl.program_id(0) == 0)
    def _():
        carry_ref[...] = jnp.zeros_like(carry_ref)

    v = z_ref[...] + bias_ref[...]
    ls = jnp.minimum(v, 0.0) - jnp.log1p(jnp.exp(-jnp.abs(v)))
    r = lax.broadcasted_iota(jnp.int32, (tm, tm), 0)
    c = lax.broadcasted_iota(jnp.int32, (tm, tm), 1)
    tri = jnp.where(c <= r, 1.0, 0.0).astype(BF16)
    tri_chunk = jnp.where((c <= r) & ((r // CHUNK) == (c // CHUNK)), 1.0, 0.0).astype(BF16)
    run = _dot_exact_lhs(tri, ls) + carry_ref[...]
    loc = _dot_exact_lhs(tri_chunk, ls)
    lane = lax.broadcasted_iota(jnp.int32, (1, 128), 1)
    out = jnp.where(lane < G_ML_I, run, jnp.where(lane < G_ML_F, v, loc))
    carry_ref[...] = run[tm - 1:tm, :]
    g_ref[...] = out
    out_t = out.T
    for ci in range(tm // CHUNK):
        gtc_ref[ci] = out_t[:32, ci * CHUNK:(ci + 1) * CHUNK]


def _gates(z, bias, tm):
    s = z.shape[0]
    return pl.pallas_call(
        _gates_body,
        grid=(s // tm,),
        in_specs=[pl.BlockSpec((tm, 128), lambda i: (i, C_GATE_128)),
                  pl.BlockSpec((1, 128), lambda i: (0, 0))],
        out_specs=[pl.BlockSpec((tm, 128), lambda i: (i, 0)),
                   pl.BlockSpec((tm // CHUNK, 32, CHUNK), lambda i: (i, 0, 0))],
        out_shape=[jax.ShapeDtypeStruct((s, 128), F32),
                   jax.ShapeDtypeStruct((s // CHUNK, 32, CHUNK), F32)],
        scratch_shapes=[pltpu.VMEM((1, 128), F32)],
        compiler_params=_cparams(("arbitrary",)),
        name="gates",
    )(z, bias)


FOX_SLOT = 128
FOX_BIAS_LANE = HEAD_DIM
LOG2E = 1.4426950408889634
FOX_MAX_STATIC_BOUND = 30.0


def _fox_prep_body(zq_ref, zk_ref, zv_ref, g_ref, shift_ref, qg_ref, kg_ref, spread_ref, place_ref,
                   q1_ref, k1_ref, q_ref, k_ref, v_ref):
    seg = _seg_matrix(MIX_W)
    q = zq_ref[...]
    k = zk_ref[...]
    q_ms = _seg_sum(q * q, seg) * (1.0 / HEAD_DIM)
    k_ms = _seg_sum(k * k, seg) * (1.0 / HEAD_DIM)
    qn = (q * lax.rsqrt(q_ms + NORM_EPS) * qg_ref[...] * (HEAD_DIM ** -0.5 * LOG2E)).astype(BF16)
    kn = (k * lax.rsqrt(k_ms + NORM_EPS) * kg_ref[...]).astype(BF16)
    spread = spread_ref[...]
    f2 = g_ref[...] * LOG2E
    f_q = sum(_dot(part, place_ref[n]) for n, part in enumerate(_split3(f2 - shift_ref[...])))
    f_k = sum(_dot(part, place_ref[3 + n]) for n, part in enumerate(_split3(f2)))
    q_ref[...] = (_dot(qn, spread) + f_q + q1_ref[...]).astype(BF16)
    k_ref[...] = (_dot(kn, spread) - f_k + k1_ref[...]).astype(BF16)
    v_ref[...] = zv_ref[...].astype(BF16)


def _fox_layout_consts():
    spread = np.zeros((MIX_W, N_HEADS * FOX_SLOT), np.float32)
    place = np.zeros((6, 128, N_HEADS * FOX_SLOT), np.float32)
    q_ones = np.zeros((1, N_HEADS * FOX_SLOT), np.float32)
    k_ones = np.zeros((1, N_HEADS * FOX_SLOT), np.float32)
    for h in range(N_HEADS):
        for dd in range(HEAD_DIM):
            spread[h * HEAD_DIM + dd, h * FOX_SLOT + dd] = 1.0
        for n in range(3):
            place[n, G_FOX_F + h, h * FOX_SLOT + FOX_BIAS_LANE + n] = 1.0
            place[3 + n, G_FOX_F + h, h * FOX_SLOT + FOX_BIAS_LANE + 3 + n] = 1.0
            k_ones[0, h * FOX_SLOT + FOX_BIAS_LANE + n] = 1.0
            q_ones[0, h * FOX_SLOT + FOX_BIAS_LANE + 3 + n] = 1.0
    return (jnp.asarray(spread, BF16), jnp.asarray(place, BF16), jnp.asarray(q_ones), jnp.asarray(k_ones))


def _fox_prep(z, g, shift, q_gain, k_gain, tm):
    s = z.shape[0]
    wide = N_HEADS * FOX_SLOT
    spread, place, q_ones, k_ones = _fox_layout_consts()
    zspec = lambda c: pl.BlockSpec((tm, MIX_W), lambda i: (i, c))
    return pl.pallas_call(
        _fox_prep_body,
        grid=(s // tm,),
        in_specs=[zspec(C_FQ), zspec(C_FK), zspec(C_FV),
                  pl.BlockSpec((tm, 128), lambda i: (i, 0)),
                  pl.BlockSpec((1, 128), lambda i: (0, 0)),
                  pl.BlockSpec((1, MIX_W), lambda i: (0, 0)),
                  pl.BlockSpec((1, MIX_W), lambda i: (0, 0)),
                  pl.BlockSpec((MIX_W, wide), lambda i: (0, 0)),
                  pl.BlockSpec((6, 128, wide), lambda i: (0, 0, 0)),
                  pl.BlockSpec((1, wide), lambda i: (0, 0)),
                  pl.BlockSpec((1, wide), lambda i: (0, 0))],
        out_specs=[pl.BlockSpec((tm, wide), lambda i: (i, 0)),
                   pl.BlockSpec((tm, wide), lambda i: (i, 0)),
                   pl.BlockSpec((tm, MIX_W), lambda i: (i, 0))],
        out_shape=[jax.ShapeDtypeStruct((s, wide), BF16),
                   jax.ShapeDtypeStruct((s, wide), BF16),
                   jax.ShapeDtypeStruct((s, MIX_W), BF16)],
        compiler_params=_cparams(("parallel",)),
        name="fox_prep",
    )(z, z, z, g, shift, q_gain, k_gain, spread, place, q_ones, k_ones)


def _fox_body(it_ref, jt_ref, q_ref, k_ref, v_ref, og_ref, o_ref, m_ref, l_ref, acc_ref):
    t = pl.program_id(1)
    i = it_ref[t]
    j = jt_ref[t]
    tq, tk = q_ref.shape[0], k_ref.shape[0]
    m0 = _lane_is_head0((1, PAIR_W))

    @pl.when(j == 0)
    def _():
        m_ref[...] = jnp.full_like(m_ref, NEG_BIG)
        l_ref[...] = jnp.zeros_like(l_ref)
        acc_ref[...] = jnp.zeros_like(acc_ref)

    def step(on_diagonal):
        v = v_ref[...]
        pvs, alphas = [], []
        for e in range(2):
            slot = slice(e * FOX_SLOT, (e + 1) * FOX_SLOT)
            s = _dot_nt(q_ref[:, slot], k_ref[:, slot])
            if on_diagonal:
                row = lax.broadcasted_iota(jnp.int32, (tq, tk), 0)
                col = lax.broadcasted_iota(jnp.int32, (tq, tk), 1)
                s = jnp.where(col <= row, s, NEG_BIG)
            m_prev = m_ref[e]
            m_new = jnp.maximum(m_prev, jnp.max(s, axis=1, keepdims=True))
            alpha = jnp.exp2(m_prev - m_new)
            pr = jnp.exp2(s - m_new)
            l_ref[e] = alpha * l_ref[e] + jnp.sum(pr, axis=1, keepdims=True)
            m_ref[e] = m_new
            pvs.append(_dot(pr.astype(BF16), v))
            alphas.append(alpha)
        acc_ref[...] = (jnp.where(m0, alphas[0], alphas[1]) * acc_ref[...]
                        + jnp.where(m0, pvs[0], pvs[1]))

    @pl.when(j < i)
    def _():
        step(False)

    @pl.when(j == i)
    def _():
        step(True)
        l_pair = jnp.where(m0, l_ref[0], l_ref[1])
        o_ref[...] = (acc_ref[...] / l_pair * _sigmoid(og_ref[...])).astype(o_ref.dtype)


def _fox_bounded_body(it_ref, jt_ref, q_ref, k_ref, v_ref, og_ref, o_ref, acc_ref):
    t = pl.program_id(1)
    i = it_ref[t]
    j = jt_ref[t]
    tq, tk = q_ref.shape[0], k_ref.shape[0]

    @pl.when(j == 0)
    def _():
        acc_ref[...] = jnp.zeros_like(acc_ref)

    n_pairs = v_ref.shape[1] // PAIR_W

    def step(on_diagonal):
        for p in range(n_pairs):
            v = v_ref[:, p * PAIR_W:(p + 1) * PAIR_W]
            v_ones = jnp.concatenate([v, jnp.ones_like(v)], axis=1)
            for e in range(2):
                h = 2 * p + e
                slot = slice(h * FOX_SLOT, (h + 1) * FOX_SLOT)
                pr = jnp.exp2(_dot_nt(q_ref[:, slot], k_ref[:, slot]))
                if on_diagonal:
                    row = lax.broadcasted_iota(jnp.int32, (tq, tk), 0)
                    col = lax.broadcasted_iota(jnp.int32, (tq, tk), 1)
                    pr = jnp.where(col <= row, pr, 0.0)
                acc_ref[h] += _dot(pr.astype(BF16), v_ones)

    @pl.when(j < i)
    def _():
        step(False)

    @pl.when(j == i)
    def _():
        step(True)
        m0 = _lane_is_head0((1, PAIR_W))
        for p in range(n_pairs):
            cols = slice(p * PAIR_W, (p + 1) * PAIR_W)
            num = jnp.where(m0, acc_ref[2 * p, :, :PAIR_W], acc_ref[2 * p + 1, :, :PAIR_W])
            den = jnp.where(m0, acc_ref[2 * p, :, PAIR_W:], acc_ref[2 * p + 1, :, PAIR_W:])
            o_ref[:, cols] = (num / den * _sigmoid(og_ref[:, cols])).astype(o_ref.dtype)


def _fox_attention(q, k, v, z, tq, bounded):
    s = v.shape[0]
    nq = s // tq
    ii, jj = np.tril_indices(nq)
    it = jnp.asarray(ii, jnp.int32)
    jt = jnp.asarray(jj, jnp.int32)
    if bounded:
        body = _fox_bounded_body
        pairs_per_step = N_PAIRS
        scratch = [pltpu.VMEM((N_HEADS, tq, 2 * PAIR_W), F32)]
    else:
        body = _fox_body
        pairs_per_step = 1
        scratch = [pltpu.VMEM((2, tq, 1), F32), pltpu.VMEM((2, tq, 1), F32),
                   pltpu.VMEM((tq, PAIR_W), F32)]
    wide = pairs_per_step * 2 * FOX_SLOT
    narrow = pairs_per_step * PAIR_W
    fo_blk = C_FO * (MIX_W // narrow)
    grid_spec = pltpu.PrefetchScalarGridSpec(
        num_scalar_prefetch=2,
        grid=(N_PAIRS // pairs_per_step, len(ii)),
        in_specs=[
            pl.BlockSpec((tq, wide), lambda p, t, it, jt: (it[t], p)),
            pl.BlockSpec((tq, wide), lambda p, t, it, jt: (jt[t], p)),
            pl.BlockSpec((tq, narrow), lambda p, t, it, jt: (jt[t], p)),
            pl.BlockSpec((tq, narrow), lambda p, t, it, jt: (it[t], fo_blk + p)),
        ],
        out_specs=pl.BlockSpec((tq, narrow), lambda p, t, it, jt: (it[t], p)),
        scratch_shapes=scratch,
    )
    return pl.pallas_call(
        body,
        grid_spec=grid_spec,
        out_shape=jax.ShapeDtypeStruct((s, MIX_W), BF16),
        compiler_params=_cparams(("parallel", "arbitrary")),
        name="fox_attention_bounded" if bounded else "fox_attention",
    )(it, jt, q, k, v, z)


def _retention_body(zq_ref, zk_ref, zv_ref, zg_ref, cos_ref, sin_ref, intra_ref, qd_ref, kd_ref,
                    cd_ref, gain_ref, o_ref, st_ref, q_scr, k_scr, y_scr):
    tm = zq_ref.shape[0]

    @pl.when(pl.program_id(0) == 0)
    def _():
        st_ref[...] = jnp.zeros_like(st_ref)

    lane = lax.broadcasted_iota(jnp.int32, (1, PAIR_W), 1)
    m0 = lane < HEAD_DIM
    first_half = (lane & (HEAD_DIM // 2)) == 0
    cos = cos_ref[...]
    sin = sin_ref[...]
    for p in range(N_PAIRS):
        cs = slice(p * PAIR_W, (p + 1) * PAIR_W)
        for src, dst, scale in ((zq_ref, q_scr, 1.0), (zk_ref, k_scr, HEAD_DIM ** -0.5)):
            x = src[:, cs]
            swapped = jnp.where(first_half, pltpu.roll(x, PAIR_W - HEAD_DIM // 2, 1),
                                pltpu.roll(x, HEAD_DIM // 2, 1))
            dst[:, cs] = (x * cos + swapped * sin) * scale
    bd, _, _ = _block_diag_mask()

    pairs = range(N_PAIRS)
    pair_cols = [slice(p * PAIR_W, (p + 1) * PAIR_W) for p in pairs]

    def chunk(c, carry):
        rows = pl.ds(pl.multiple_of(c * CHUNK, CHUNK), CHUNK)
        q = [q_scr[rows, cols] for cols in pair_cols]
        k = [k_scr[rows, cols] for cols in pair_cols]
        vb = [zv_ref[rows, cols].astype(BF16) for cols in pair_cols]
        st = [st_ref[p] for p in pairs]
        sc = [_dot_nt(_stack_heads(q[p], m0).astype(BF16), k[p].astype(BF16)) * intra_ref[p]
              for p in pairs]
        o_inter = [_dot((q[p] * qd_ref[p]).astype(BF16), st[p].astype(BF16)) for p in pairs]
        upd = [_dot_tn((k[p] * kd_ref[p]).astype(BF16), vb[p]) for p in pairs]
        o_intra = [_unstack_rows(_dot(sc[p].astype(BF16), vb[p]), m0) for p in pairs]
        for p in pairs:
            y_scr[rows, pair_cols[p]] = o_intra[p] + o_inter[p]
            st_ref[p] = st[p] * cd_ref[p] + jnp.where(bd, upd[p], 0.0)
        return carry

    lax.fori_loop(0, tm // CHUNK, chunk, 0)
    y = y_scr[...]
    ms = _seg_sum(y * y, _seg_matrix(MIX_W)) * (1.0 / HEAD_DIM)
    o_ref[...] = (y * lax.rsqrt(ms + NORM_EPS) * gain_ref[...] * _silu(zg_ref[...])).astype(o_ref.dtype)


def _retention_tables():
    hh = jnp.arange(N_HEADS, dtype=F32)
    log_gamma = jnp.log1p(-jnp.exp2(-5.0 - hh))
    idx = jnp.arange(CHUNK, dtype=F32)
    intra = jnp.exp(log_gamma[:, None, None] * jnp.abs(idx[:, None] - idx[None, :]))
    q_decay = jnp.exp(log_gamma[:, None] * (idx + 1.0))
    k_decay = jnp.exp(log_gamma[:, None] * (CHUNK - 1.0 - idx))
    chunk_decay = jnp.exp(log_gamma * CHUNK)
    intra_st = intra.reshape(N_PAIRS, 2 * CHUNK, CHUNK)
    to_pair = lambda t: jnp.repeat(t.reshape(N_PAIRS, 2, CHUNK).transpose(0, 2, 1), HEAD_DIM, axis=2)
    cd = jnp.repeat(chunk_decay.reshape(N_PAIRS, 1, 2), HEAD_DIM, axis=2)
    return intra_st, to_pair(q_decay), to_pair(k_decay), cd


def _retention(z, cos, sin, gain, tm):
    s = z.shape[0]
    intra_st, qd, kd, cd = _retention_tables()
    zspec = lambda c: pl.BlockSpec((tm, MIX_W), lambda i: (i, c))
    full = lambda a: pl.BlockSpec(a.shape, lambda i: (0,) * a.ndim)
    return pl.pallas_call(
        _retention_body,
        grid=(s // tm,),
        in_specs=[zspec(C_TQ), zspec(C_TK), zspec(C_TV), zspec(C_TG),
                  pl.BlockSpec((tm, PAIR_W), lambda i: (i, 0)),
                  pl.BlockSpec((tm, PAIR_W), lambda i: (i, 0)),
                  full(intra_st), full(qd), full(kd), full(cd),
                  pl.BlockSpec((1, MIX_W), lambda i: (0, 0))],
        out_specs=pl.BlockSpec((tm, MIX_W), lambda i: (i, 0)),
        out_shape=jax.ShapeDtypeStruct((s, MIX_W), BF16),
        scratch_shapes=[pltpu.VMEM((N_PAIRS, PAIR_W, PAIR_W), F32),
                        pltpu.VMEM((tm, MIX_W), F32), pltpu.VMEM((tm, MIX_W), F32),
                        pltpu.VMEM((tm, MIX_W), F32)],
        compiler_params=_cparams(("arbitrary",)),
        name="retention",
    )(z, z, z, z, cos, sin, intra_st, qd, kd, cd, gain)


def _load_with_halo(ext_ref, cur_ref, halo_ref, first):
    halo = halo_ref[...]
    ext_ref[0:HALO, :] = jnp.where(first, jnp.zeros_like(halo), halo)
    ext_ref[HALO:, :] = cur_ref[...]


def _mlstm_body(zq_ref, zk_ref, hq_ref, hk_ref, zv_ref, zo_ref, g_ref, gtc_ref, cw_ref, cb_ref,
                gain_ref, o_ref, c_ref, n_ref, m_ref, ext_scr, q_scr, k_scr, y_scr):
    tm = zq_ref.shape[0]
    first = pl.program_id(0) == 0

    @pl.when(first)
    def _():
        c_ref[...] = jnp.zeros_like(c_ref)
        n_ref[...] = jnp.zeros_like(n_ref)
        m_ref[...] = jnp.zeros_like(m_ref)

    for part, (cur, halo, dst, scale) in enumerate(((zq_ref, hq_ref, q_scr, 1.0),
                                                    (zk_ref, hk_ref, k_scr, HEAD_DIM ** -0.5))):
        _load_with_halo(ext_scr, cur, halo, first)
        cols = slice(part * MIX_W, (part + 1) * MIX_W)
        acc = cb_ref[:, cols]
        for tap in range(CONV_W):
            off = HALO - (CONV_W - 1) + tap
            acc = acc + cw_ref[tap:tap + 1, cols] * ext_scr[off:off + tm, :]
        dst[...] = _silu(acc) * scale

    lane = lax.broadcasted_iota(jnp.int32, (1, PAIR_W), 1)
    m0 = lane < HEAD_DIM
    bd, _, _ = _block_diag_mask()
    ri = lax.broadcasted_iota(jnp.int32, (CHUNK, CHUNK), 0)
    ci = lax.broadcasted_iota(jnp.int32, (CHUNK, CHUNK), 1)
    causal = ci <= ri

    pairs = range(N_PAIRS)
    pair_cols = [slice(p * PAIR_W, (p + 1) * PAIR_W) for p in pairs]
    pair = lambda t0, t1: jnp.where(m0, t0, t1)

    def head_gates(g, gt, m_pair, h, e):
        b_c = g[:, G_ML_F + h:G_ML_F + h + 1]
        li_c = g[:, G_ML_I + h:G_ML_I + h + 1]
        b_r = gt[G_ML_F + h:G_ML_F + h + 1, :]
        li_r = gt[G_ML_I + h:G_ML_I + h + 1, :]
        m_prev = m_pair[:, e * HEAD_DIM:e * HEAD_DIM + 1]
        log_d = jnp.where(causal, b_c - b_r + li_r, NEG_BIG)
        log_inter = b_c + m_prev
        m_t = jnp.maximum(log_inter, jnp.max(log_d, axis=1, keepdims=True))
        b_last = b_c[CHUNK - 1:CHUNK, :]
        log_w = b_last - b_c + li_c
        m_new = jnp.maximum(b_last + m_prev, jnp.max(log_w, axis=0, keepdims=True))
        return dict(d_mat=jnp.exp(log_d - m_t), inter=jnp.exp(log_inter - m_t), floor=jnp.exp(-m_t),
                    w=jnp.exp(log_w - m_new), dec=jnp.exp(b_last + m_prev - m_new), m_new=m_new)

    def chunk(c, carry):
        rows = pl.ds(pl.multiple_of(c * CHUNK, CHUNK), CHUNK)
        g = g_ref[rows, :]
        gt = gtc_ref[c]
        q = [q_scr[rows, cols] for cols in pair_cols]
        k = [k_scr[rows, cols] for cols in pair_cols]
        vb = [zv_ref[rows, cols].astype(BF16) for cols in pair_cols]
        cst = [c_ref[p] for p in pairs]
        nst = [n_ref[p] for p in pairs]
        hg = [[head_gates(g, gt, m_ref[p], 2 * p + e, e) for e in range(2)] for p in pairs]
        sc = [_dot_nt(_stack_heads(q[p], m0).astype(BF16), k[p].astype(BF16)) for p in pairs]
        q_c = [_dot(q[p].astype(BF16), cst[p].astype(BF16)) for p in pairs]
        sc = [sc[p] * jnp.concatenate([hg[p][0]["d_mat"], hg[p][1]["d_mat"]], axis=0) for p in pairs]
        num = [_unstack_rows(_dot(sc[p].astype(BF16), vb[p]), m0) for p in pairs]
        den_intra = [jnp.sum(sc[p], axis=1, keepdims=True) for p in pairs]
        kw = [k[p] * pair(hg[p][0]["w"], hg[p][1]["w"]) for p in pairs]
        upd = [_dot_tn(kw[p].astype(BF16), vb[p]) for p in pairs]
        for p in pairs:
            h0, h1 = hg[p]
            num_p = num[p] + pair(h0["inter"], h1["inter"]) * q_c[p]
            qn = q[p] * nst[p]
            zqn = jnp.zeros_like(qn)
            qn0 = jnp.sum(jnp.where(m0, qn, zqn), axis=1, keepdims=True)
            qn1 = jnp.sum(jnp.where(m0, zqn, qn), axis=1, keepdims=True)
            den0 = den_intra[p][:CHUNK] + h0["inter"] * qn0
            den1 = den_intra[p][CHUNK:] + h1["inter"] * qn1
            denom = pair(jnp.maximum(jnp.abs(den0), h0["floor"]), jnp.maximum(jnp.abs(den1), h1["floor"]))
            y_scr[rows, pair_cols[p]] = num_p / denom
            dec_pair = pair(h0["dec"], h1["dec"])
            c_ref[p] = dec_pair * cst[p] + jnp.where(bd, upd[p], 0.0)
            n_ref[p] = dec_pair * nst[p] + jnp.sum(kw[p], axis=0, keepdims=True)
            m_ref[p] = pair(h0["m_new"], h1["m_new"])
        return carry

    lax.fori_loop(0, tm // CHUNK, chunk, 0)
    y = y_scr[...] * _sigmoid(zo_ref[...])
    ms = _seg_sum(y * y, _seg_matrix(MIX_W)) * (1.0 / HEAD_DIM)
    o_ref[...] = (y * lax.rsqrt(ms + NORM_EPS) * gain_ref[...]).astype(o_ref.dtype)


def _halo_spec(tm, width, c):
    step = tm // HALO
    return pl.BlockSpec((HALO, width), lambda i: (jnp.maximum(i * step - 1, 0), c))


def _mlstm(z, g, gtc, conv_w, conv_b, gain, tm):
    s = z.shape[0]
    zspec = lambda c: pl.BlockSpec((tm, MIX_W), lambda i: (i, c))
    return pl.pallas_call(
        _mlstm_body,
        grid=(s // tm,),
        in_specs=[zspec(C_MQ), zspec(C_MK), _halo_spec(tm, MIX_W, C_MQ), _halo_spec(tm, MIX_W, C_MK),
                  zspec(C_MV), zspec(C_MO),
                  pl.BlockSpec((tm, 128), lambda i: (i, 0)),
                  pl.BlockSpec((tm // CHUNK, 32, CHUNK), lambda i: (i, 0, 0)),
                  pl.BlockSpec((CONV_W, 2 * MIX_W), lambda i: (0, 0)),
                  pl.BlockSpec((1, 2 * MIX_W), lambda i: (0, 0)),
                  pl.BlockSpec((1, MIX_W), lambda i: (0, 0))],
        out_specs=pl.BlockSpec((tm, MIX_W), lambda i: (i, 0)),
        out_shape=jax.ShapeDtypeStruct((s, MIX_W), BF16),
        scratch_shapes=[pltpu.VMEM((N_PAIRS, PAIR_W, PAIR_W), F32),
                        pltpu.VMEM((N_PAIRS, 1, PAIR_W), F32),
                        pltpu.VMEM((N_PAIRS, 1, PAIR_W), F32),
                        pltpu.VMEM((tm + HALO, MIX_W), F32),
                        pltpu.VMEM((tm, MIX_W), F32), pltpu.VMEM((tm, MIX_W), F32),
                        pltpu.VMEM((tm, MIX_W), F32)],
        compiler_params=_cparams(("arbitrary",)),
        name="mlstm",
    )(z, z, z, z, z, z, g, gtc, conv_w, conv_b, gain)


def _rwkv_body(zr_ref, zk_ref, zv_ref, zl_ref, hr_ref, hk_ref, hv_ref, hl_ref, mu_ref, mul_ref,
               wl_ref, w0_ref, a0_ref, kk_ref, ka_ref, rk_ref, gnw_ref, gnb_ref, o_ref,
               h_ref, ext_scr, extl_scr, rt_scr, kt_scr, kb_scr, bb_scr, kh_scr, bh_scr,
               gam_scr, v_scr, y_scr):
    tm = zr_ref.shape[0]
    first = pl.program_id(0) == 0

    @pl.when(first)
    def _():
        h_ref[...] = jnp.zeros_like(h_ref)

    def shifted(ext, cur_ref, halo_ref, mu):
        _load_with_halo(ext, cur_ref, halo_ref, first)
        cur = ext[HALO:HALO + tm, :]
        prev = ext[HALO - 1:HALO - 1 + tm, :]
        return cur + mu * (prev - cur)

    r = shifted(ext_scr, zr_ref, hr_ref, mu_ref[:, 0:MIX_W])
    k = shifted(ext_scr, zk_ref, hk_ref, mu_ref[:, MIX_W:2 * MIX_W])
    v = shifted(ext_scr, zv_ref, hv_ref, mu_ref[:, 2 * MIX_W:3 * MIX_W])
    lora_in = shifted(extl_scr, zl_ref, hl_ref, mul_ref[...])
    lane_l = lax.broadcasted_iota(jnp.int32, (1, LORA_W), 1)
    act = jnp.where(lane_l < 64, jnp.tanh(lora_in), jnp.where(lane_l < 128, lora_in, _sigmoid(lora_in)))
    lora = _dot(act.astype(BF16), wl_ref[...])
    logw = -_sigmoid(w0_ref[...] + lora[:, 0:MIX_W]) * math.exp(-0.5)
    a = _sigmoid(a0_ref[...] + lora[:, MIX_W:2 * MIX_W])
    gate = lora[:, 2 * MIX_W:3 * MIX_W]

    seg = _seg_matrix(MIX_W)
    kk = k * kk_ref[...]
    kk = kk * lax.rsqrt(jnp.maximum(_seg_sum(kk * kk, seg), 1e-12))
    k2 = k * (1.0 + (a - 1.0) * ka_ref[...])
    b = kk * a
    bonus = _seg_sum(r * k2 * rk_ref[...], seg) * v

    ri = lax.broadcasted_iota(jnp.int32, (tm, tm), 0)
    ci = lax.broadcasted_iota(jnp.int32, (tm, tm), 1)
    same = (ri // CHUNK) == (ci // CHUNK)
    lower = jnp.where(same & (ci <= ri), 1.0, 0.0).astype(BF16)
    upper = jnp.where(same & (ci > ri), 1.0, 0.0).astype(BF16)
    cl = _dot_exact_lhs(lower, logw)
    cs = _dot_exact_lhs(upper, logw)
    e_cl = jnp.exp(cl)
    e_ncl = jnp.exp(-cl)
    e_cs = jnp.exp(cs)
    rt_scr[...] = r * e_cl
    kt_scr[...] = kk * jnp.exp(cl - logw)
    kb_scr[...] = k2 * e_ncl
    bb_scr[...] = b * e_ncl
    kh_scr[...] = k2 * e_cs
    bh_scr[...] = b * e_cs
    gam_scr[...] = jnp.exp(cl + cs)
    v_scr[...] = v

    lane = lax.broadcasted_iota(jnp.int32, (1, PAIR_W), 1)
    m0 = lane < HEAD_DIM
    bd, rr, cc = _block_diag_mask()
    strict = bd & (cc < rr)
    incl = bd & (cc <= rr)
    eye = bd & (cc == rr)
    eye_f = jnp.where(eye, 1.0, 0.0)

    pairs = range(N_PAIRS)
    pair_cols = [slice(p * PAIR_W, (p + 1) * PAIR_W) for p in pairs]

    def chunk(c, carry):
        rows = pl.ds(pl.multiple_of(c * CHUNK, CHUNK), CHUNK)
        first8 = pl.ds(pl.multiple_of(c * CHUNK, CHUNK), 8)
        stacked = lambda scr: [_stack_heads(scr[rows, cols], m0) for cols in pair_cols]
        rt_st = stacked(rt_scr)
        kt_st = stacked(kt_scr)
        v_st = [t.astype(BF16) for t in stacked(v_scr)]
        kh_st = [t.astype(BF16) for t in stacked(kh_scr)]
        bh_st = [t.astype(BF16) for t in stacked(bh_scr)]
        twice = lambda scr: [jnp.concatenate([scr[rows, cols].astype(BF16)] * 2, axis=0)
                             for cols in pair_cols]
        bb2 = twice(bb_scr)
        kb2 = twice(kb_scr)
        gam = [gam_scr[first8, cols][0:1] for cols in pair_cols]
        lhs = [jnp.concatenate([kt_st[p], rt_st[p]], axis=0).astype(BF16) for p in pairs]
        g_b = [_dot_nt(lhs[p], bb2[p]) for p in pairs]
        g_k = [_dot_nt(lhs[p], kb2[p]) for p in pairs]
        x = [jnp.where(strict, -g_b[p][:PAIR_W], 0.0) for p in pairs]
        q_bd = [jnp.where(incl, g_b[p][PAIR_W:], 0.0).astype(BF16) for p in pairs]
        b_bd = [jnp.where(strict, g_k[p][:PAIR_W], 0.0).astype(BF16) for p in pairs]
        p_bd = [jnp.where(incl, g_k[p][PAIR_W:], 0.0).astype(BF16) for p in pairs]
        bv = [_dot(b_bd[p], v_st[p]).astype(BF16) for p in pairs]
        pv = [_dot(p_bd[p], v_st[p]) for p in pairs]
        khv = [_dot_tn(kh_st[p], v_st[p]) for p in pairs]
        t_inv = [eye_f + x[p] for p in pairs]
        for _ in range(5):
            xb = [x[p].astype(BF16) for p in pairs]
            x = [_dot(xb[p], xb[p]) for p in pairs]
            t_inv = [t_inv[p] + _dot(t_inv[p].astype(BF16), x[p].astype(BF16)) for p in pairs]
        tb = [t_inv[p].astype(BF16) for p in pairs]
        kt2 = [_dot(tb[p], kt_st[p].astype(BF16)).astype(BF16) for p in pairs]
        w1 = [_dot(tb[p], bv[p]).astype(BF16) for p in pairs]
        y1 = [(rt_st[p] - _dot(q_bd[p], kt2[p])).astype(BF16) for p in pairs]
        y0 = [pv[p] - _dot(q_bd[p], w1[p]) for p in pairs]
        m_mat = [(jnp.where(eye, gam[p], 0.0) - _dot_tn(bh_st[p], kt2[p])).astype(BF16) for p in pairs]
        n_mat = [khv[p] - _dot_tn(bh_st[p], w1[p]) for p in pairs]
        hbb = [h_ref[p].astype(BF16) for p in pairs]
        y_st = [_dot(y1[p], hbb[p]) + y0[p] for p in pairs]
        h_new = [_dot(m_mat[p], hbb[p]) + n_mat[p] for p in pairs]
        for p in pairs:
            y_scr[rows, pair_cols[p]] = y_st[p][:CHUNK] + y_st[p][CHUNK:]
            h_ref[p] = h_new[p]
        return carry

    lax.fori_loop(0, tm // CHUNK, chunk, 0)

    y = y_scr[...]
    mean = _seg_sum(y, seg) * (1.0 / HEAD_DIM)
    yc = y - mean
    var = _seg_sum(yc * yc, seg) * (1.0 / HEAD_DIM)
    yn = yc * lax.rsqrt(var + RWKV_GN_EPS) * gnw_ref[...] + gnb_ref[...]
    o_ref[...] = ((yn + bonus) * gate).astype(o_ref.dtype)


def _rwkv(z, mu, mu_l, w_lora, w0, a0, k_k, k_a, r_k, gn_w, gn_b, tm):
    s = z.shape[0]
    zspec = lambda c: pl.BlockSpec((tm, MIX_W), lambda i: (i, c))
    row = lambda w: pl.BlockSpec((1, w), lambda i: (0, 0))
    big = pltpu.VMEM((tm, MIX_W), F32)
    return pl.pallas_call(
        _rwkv_body,
        grid=(s // tm,),
        in_specs=[zspec(C_RR), zspec(C_RK), zspec(C_RV),
                  pl.BlockSpec((tm, LORA_W), lambda i: (i, C_RL_256)),
                  _halo_spec(tm, MIX_W, C_RR), _halo_spec(tm, MIX_W, C_RK), _halo_spec(tm, MIX_W, C_RV),
                  _halo_spec(tm, LORA_W, C_RL_256),
                  row(3 * MIX_W), row(LORA_W),
                  pl.BlockSpec((LORA_W, 3 * MIX_W), lambda i: (0, 0)),
                  row(MIX_W), row(MIX_W), row(MIX_W), row(MIX_W), row(MIX_W), row(MIX_W), row(MIX_W)],
        out_specs=pl.BlockSpec((tm, MIX_W), lambda i: (i, 0)),
        out_shape=jax.ShapeDtypeStruct((s, MIX_W), BF16),
        scratch_shapes=[pltpu.VMEM((N_PAIRS, PAIR_W, PAIR_W), F32),
                        pltpu.VMEM((tm + HALO, MIX_W), F32), pltpu.VMEM((tm + HALO, LORA_W), F32),
                        big, big, big, big, big, big, big, big, big],
        compiler_params=_cparams(("arbitrary",)),
        name="rwkv7",
    )(z, z, z, z, z, z, z, z, mu, mu_l, w_lora, w0, a0, k_k, k_a, r_k, gn_w, gn_b)


def _w_in_column_map():
    fox, rwkv, ret, ml = 0, 2056, 3848, 5896
    src = np.full((Z_COLS,), -1, np.int64)

    def put(dst, start, n):
        src[dst:dst + n] = np.arange(start, start + n)

    for blk, start in ((C_FQ, fox), (C_FK, fox + 512), (C_FV, fox + 1024), (C_FO, fox + 1536),
                       (C_RR, rwkv), (C_RK, rwkv + 512), (C_RV, rwkv + 1024),
                       (C_TQ, ret), (C_TK, ret + 512), (C_TV, ret + 1024), (C_TG, ret + 1536),
                       (C_MQ, ml), (C_MK, ml + 512), (C_MV, ml + 1024), (C_MO, ml + 1536)):
        put(blk * MIX_W, start, MIX_W)
    put(C_RL_256 * LORA_W, rwkv + 1536, LORA_W)
    gate0 = C_GATE_128 * 128
    put(gate0 + G_FOX_F, fox + 2048, N_HEADS)
    put(gate0 + G_ML_I, ml + 2048, N_HEADS)
    put(gate0 + G_ML_F, ml + 2056, N_HEADS)
    return src


def _permute_w_in(w_in):
    src = _w_in_column_map()
    parts, start = [], 0
    for pos in range(1, Z_COLS + 1):
        run_continues = pos < Z_COLS and (
            (src[pos] < 0 and src[pos - 1] < 0) or (src[pos - 1] >= 0 and src[pos] == src[pos - 1] + 1))
        if not run_continues:
            n = pos - start
            if src[start] < 0:
                parts.append(jnp.zeros((w_in.shape[0], n), BF16))
            else:
                parts.append(w_in[:, src[start]:src[start] + n].astype(BF16))
            start = pos
    return jnp.concatenate(parts, axis=1)


def kernel(x, positions, ffn1_norm, ffn1_w1, ffn1_w3, ffn1_w2, mix_norm, w_in, fox_f_bias, fox_q_gain, fox_k_gain, rwkv_shift_mu, rwkv_w0, rwkv_w2, rwkv_a0, rwkv_a2, rwkv_g2, rwkv_k_k, rwkv_k_a, rwkv_r_k, rwkv_gn_w, rwkv_gn_b, ret_gn_gain, mlstm_conv_w, mlstm_conv_b, mlstm_i_bias, mlstm_f_bias, mlstm_gn_gain, w_merge_gate, merge_gate_bias, w_branch, w_out, ffn2_norm, ffn2_w1, ffn2_w3, ffn2_w2):
    batch, seq, d = x.shape
    depth = w_in.shape[0]
    dff = ffn1_w1.shape[-1]
    pick = lambda full, want: want if full % want == 0 else full
    tm_dense = pick(seq, 512)
    tm_inproj = pick(seq, 1024)
    tf = pick(dff, 512)
    te = pick(d, 256)
    tn = pick(Z_COLS, 1024)
    tm_row = pick(seq, 512)
    tm_seq = pick(seq, 256)
    tq = pick(seq, 512)
    row = lambda t: t.reshape(1, -1).astype(F32)

    outs = []
    for bi in range(batch):
        xb = x[bi]
        pos_col = positions[bi].astype(F32).reshape(seq, 1)
        cos, sin = _rope_tables(pos_col, tm_row)
        for l in range(depth):
            xb = _ffn(xb, row(ffn1_norm[l]), ffn1_w1[l].astype(BF16), ffn1_w3[l].astype(BF16),
                      ffn1_w2[l].astype(BF16), tm_dense, tf)
            z = _inproj(xb, row(mix_norm[l]), _permute_w_in(w_in[l]), tm_inproj, tn)

            gate_bias = jnp.zeros((128,), F32)
            gate_bias = gate_bias.at[G_FOX_F:G_FOX_F + N_HEADS].set(fox_f_bias[l])
            gate_bias = gate_bias.at[G_ML_I:G_ML_I + N_HEADS].set(mlstm_i_bias[l])
            gate_bias = gate_bias.at[G_ML_F:G_ML_F + N_HEADS].set(mlstm_f_bias[l])
            g, gtc = _gates(z, gate_bias[None, :], tm_row)

            bound = (LOG2E * HEAD_DIM ** 0.5) * jnp.max(jnp.abs(fox_q_gain[l])) * jnp.max(jnp.abs(fox_k_gain[l]))
            bounded = bound <= FOX_MAX_STATIC_BOUND
            shift = jnp.where(bounded, bound, 0.0) * jnp.ones((1, 128), F32)
            fq, fk, fv = _fox_prep(z, g, shift, row(jnp.tile(fox_q_gain[l], N_HEADS)),
                                   row(jnp.tile(fox_k_gain[l], N_HEADS)), tm_row)
            y_fox = lax.cond(bounded,
                             lambda *a: _fox_attention(*a, tq, True),
                             lambda *a: _fox_attention(*a, tq, False), fq, fk, fv, z)

            mu = rwkv_shift_mu[l]
            w_lora = jnp.zeros((LORA_W, 3 * MIX_W), F32)
            w_lora = w_lora.at[0:64, 0:MIX_W].set(rwkv_w2[l])
            w_lora = w_lora.at[64:128, MIX_W:2 * MIX_W].set(rwkv_a2[l])
            w_lora = w_lora.at[128:256, 2 * MIX_W:3 * MIX_W].set(rwkv_g2[l])
            y_rwkv = _rwkv(z, row(mu[:3 * MIX_W]), row(mu[3 * MIX_W:]), w_lora.astype(BF16),
                           row(rwkv_w0[l]), row(rwkv_a0[l]), row(rwkv_k_k[l]), row(rwkv_k_a[l]),
                           row(rwkv_r_k[l]), row(rwkv_gn_w[l]), row(rwkv_gn_b[l]), tm_seq)

            y_ret = _retention(z, cos, sin, row(ret_gn_gain[l]), tm_seq)

            y_m = _mlstm(z, g, gtc, mlstm_conv_w[l].astype(F32), row(mlstm_conv_b[l]),
                         row(mlstm_gn_gain[l]), tm_seq)

            xb = _merge(xb, row(mix_norm[l]), (y_fox, y_rwkv, y_ret, y_m),
                        w_merge_gate[l].reshape(d, -1).astype(BF16), merge_gate_bias[l].astype(F32),
                        w_branch[l].astype(BF16), w_out[l].astype(BF16), tm_dense, te)
            xb = _ffn(xb, row(ffn2_norm[l]), ffn2_w1[l].astype(BF16), ffn2_w3[l].astype(BF16),
                      ffn2_w2[l].astype(BF16), tm_dense, tf)
        outs.append(xb)
    return jnp.stack(outs, axis=0)
```

```python
import functools
import math

import numpy as np
import jax
import jax.numpy as jnp
from jax import lax
from jax.experimental import pallas as pl
from jax.experimental.pallas import tpu as pltpu

F32 = jnp.float32
BF16 = jnp.bfloat16

N_HEADS = 8
HEAD_DIM = 64
MIX_W = N_HEADS * HEAD_DIM
N_PAIRS = N_HEADS // 2
PAIR_W = 2 * HEAD_DIM
CHUNK = 64
CONV_W = 4
LORA_W = 256
ROPE_BASE = 10000.0
NORM_EPS = 1e-6
RWKV_GN_EPS = 64e-5
NEG_BIG = -1e30
HALO = 8

Z_COLS = 8192
(C_FQ, C_FK, C_FV, C_FO, C_RR, C_RK, C_RV, C_TQ, C_TK, C_TV, C_TG,
 C_MQ, C_MK, C_MV, C_MO) = range(15)
C_RL_256 = 7680 // LORA_W
C_GATE_128 = 7936 // 128
G_FOX_F, G_ML_I, G_ML_F = 0, 8, 16

VMEM_LIMIT = 56 * 1024 * 1024
CAST_BLOCK_ELEMS = 1 << 20


def _cparams(sem):
    return pltpu.CompilerParams(dimension_semantics=sem, vmem_limit_bytes=VMEM_LIMIT)


def _dot(a, b):
    return jnp.dot(a, b, preferred_element_type=F32)


def _dot_nt(a, b):
    return lax.dot_general(a, b, (((1,), (1,)), ((), ())), preferred_element_type=F32)


def _dot_tn(a, b):
    return lax.dot_general(a, b, (((0,), (0,)), ((), ())), preferred_element_type=F32)


def _split2(x):
    hi = x.astype(BF16)
    lo = (x - hi.astype(F32)).astype(BF16)
    return hi, lo


def _split3(x):
    hi = x.astype(BF16)
    r = x - hi.astype(F32)
    mid = r.astype(BF16)
    lo = (r - mid.astype(F32)).astype(BF16)
    return hi, mid, lo


def _dot_exact_lhs(m01, x):
    hi, mid, lo = _split3(x)
    return _dot(m01, hi) + _dot(m01, mid) + _dot(m01, lo)


def _dot_exact_rhs(x, m01):
    hi, mid, lo = _split3(x)
    return _dot(hi, m01) + _dot(mid, m01) + _dot(lo, m01)


def _seg_matrix(n):
    r = lax.broadcasted_iota(jnp.int32, (n, n), 0) // HEAD_DIM
    c = lax.broadcasted_iota(jnp.int32, (n, n), 1) // HEAD_DIM
    return jnp.where(r == c, 1.0, 0.0).astype(BF16)


def _seg_sum(x, seg):
    hi, lo = _split2(x)
    return _dot(hi, seg) + _dot(lo, seg)


def _sigmoid(x):
    return 1.0 / (1.0 + jnp.exp(-x))


def _silu(x):
    return x * _sigmoid(x)


def _rms_rows(x, gain):
    ms = jnp.mean(x * x, axis=-1, keepdims=True)
    return x * lax.rsqrt(ms + NORM_EPS) * gain


def _lane_is_head0(shape):
    return lax.broadcasted_iota(jnp.int32, shape, len(shape) - 1) < HEAD_DIM


def _stack_heads(x, m0):
    z = jnp.zeros_like(x)
    return jnp.concatenate([jnp.where(m0, x, z), jnp.where(m0, z, x)], axis=0)


def _unstack_rows(x_st, m0):
    return jnp.where(m0, x_st[:CHUNK], x_st[CHUNK:])


def _block_diag_mask():
    r = lax.broadcasted_iota(jnp.int32, (PAIR_W, PAIR_W), 0)
    c = lax.broadcasted_iota(jnp.int32, (PAIR_W, PAIR_W), 1)
    return (r // HEAD_DIM) == (c // HEAD_DIM), r % HEAD_DIM, c % HEAD_DIM


def _ffn_body(x_ref, g_ref, w1_ref, w3_ref, w2_ref, o_ref, h_ref):
    @pl.when(pl.program_id(1) == 0)
    def _():
        x = x_ref[...]
        h_ref[...] = _rms_rows(x, g_ref[...]).astype(BF16)
        o_ref[...] = x

    h = h_ref[...]
    half = w1_ref.shape[1] // 2
    parts = []
    for c in range(2):
        cols = slice(c * half, (c + 1) * half)
        a = _dot(h, w1_ref[:, cols])
        b = _dot(h, w3_ref[:, cols])
        g = (0.5 * _silu(a)) * b
        parts.append(_dot(g.astype(BF16), w2_ref[cols, :]))
    o_ref[...] += parts[0] + parts[1]


def _ffn(x, gain, w1, w3, w2, tm, tf):
    s, d = x.shape
    dff = w1.shape[1]
    return pl.pallas_call(
        _ffn_body,
        grid=(s // tm, dff // tf),
        in_specs=[
            pl.BlockSpec((tm, d), lambda i, j: (i, 0)),
            pl.BlockSpec((1, d), lambda i, j: (0, 0)),
            pl.BlockSpec((d, tf), lambda i, j: (0, j)),
            pl.BlockSpec((d, tf), lambda i, j: (0, j)),
            pl.BlockSpec((tf, d), lambda i, j: (j, 0)),
        ],
        out_specs=pl.BlockSpec((tm, d), lambda i, j: (i, 0)),
        out_shape=jax.ShapeDtypeStruct((s, d), F32),
        scratch_shapes=[pltpu.VMEM((tm, d), BF16)],
        compiler_params=_cparams(("parallel", "arbitrary")),
        name="ffn",
    )(x, gain, w1, w3, w2)


def _inproj_body(x_ref, g_ref, w_ref, o_ref, h_ref):
    @pl.when(pl.program_id(1) == 0)
    def _():
        h_ref[...] = _rms_rows(x_ref[...], g_ref[...]).astype(BF16)

    o_ref[...] = _dot(h_ref[...], w_ref[...])


def _inproj(x, gain, w, tm, tn):
    s, d = x.shape
    n = w.shape[1]
    return pl.pallas_call(
        _inproj_body,
        grid=(s // tm, n // tn),
        in_specs=[
            pl.BlockSpec((tm, d), lambda i, j: (i, 0)),
            pl.BlockSpec((1, d), lambda i, j: (0, 0)),
            pl.BlockSpec((d, tn), lambda i, j: (0, j)),
        ],
        out_specs=pl.BlockSpec((tm, tn), lambda i, j: (i, j)),
        out_shape=jax.ShapeDtypeStruct((s, n), F32),
        scratch_shapes=[pltpu.VMEM((tm, d), BF16)],
        compiler_params=_cparams(("parallel", "arbitrary")),
        name="inproj",
    )(x, gain, w)


def _merge_body(x_ref, g_ref, y0, y1, y2, y3, wg0, wg1, wg2, wg3, bg_ref, wb_ref, wo_ref,
                o_ref, h_ref):
    @pl.when(pl.program_id(1) == 0)
    def _():
        x = x_ref[...]
        h_ref[...] = _rms_rows(x, g_ref[...]).astype(BF16)
        o_ref[...] = x

    h = h_ref[...]
    merged = None
    for n, (y_ref, wg_ref) in enumerate(((y0, wg0), (y1, wg1), (y2, wg2), (y3, wg3))):
        gate = _sigmoid(_dot(h, wg_ref[...]) + bg_ref[n:n + 1, :])
        term = gate * _dot(y_ref[...], wb_ref[n])
        merged = term if merged is None else merged + term
    o_ref[...] += _dot(merged.astype(BF16), wo_ref[...])


def _merge(x, gain, ys, wg, bg, wb, wo, tm, te):
    s, d = x.shape
    nb = len(ys)
    ne = d // te
    y_specs = [pl.BlockSpec((tm, MIX_W), lambda i, j: (i, 0)) for _ in range(nb)]
    wg_specs = [pl.BlockSpec((d, te), functools.partial(lambda i, j, n: (0, n * ne + j), n=n))
                for n in range(nb)]
    return pl.pallas_call(
        _merge_body,
        grid=(s // tm, ne),
        in_specs=[
            pl.BlockSpec((tm, d), lambda i, j: (i, 0)),
            pl.BlockSpec((1, d), lambda i, j: (0, 0)),
            *y_specs, *wg_specs,
            pl.BlockSpec((nb, te), lambda i, j: (0, j)),
            pl.BlockSpec((nb, MIX_W, te), lambda i, j: (0, 0, j)),
            pl.BlockSpec((te, d), lambda i, j: (j, 0)),
        ],
        out_specs=pl.BlockSpec((tm, d), lambda i, j: (i, 0)),
        out_shape=jax.ShapeDtypeStruct((s, d), F32),
        scratch_shapes=[pltpu.VMEM((tm, d), BF16)],
        compiler_params=_cparams(("parallel", "arbitrary")),
        name="merge",
    )(x, gain, *ys, wg, wg, wg, wg, bg, wb, wo)


def _rope_body(pos_ref, inv_ref, sgn_ref, cos_ref, sin_ref):
    ang = pos_ref[...] * inv_ref[...]
    cos_ref[...] = jnp.cos(ang)
    sin_ref[...] = jnp.sin(ang) * sgn_ref[...]


def _rope_tables(pos_col, tm):
    s = pos_col.shape[0]
    half = HEAD_DIM // 2
    inv = ROPE_BASE ** (-jnp.arange(0, HEAD_DIM, 2, dtype=F32) / HEAD_DIM)
    inv = jnp.tile(inv, PAIR_W // half)[None, :]
    sgn = jnp.tile(jnp.concatenate([-jnp.ones((half,), F32), jnp.ones((half,), F32)]), 2)[None, :]
    return pl.pallas_call(
        _rope_body,
        grid=(s // tm,),
        in_specs=[pl.BlockSpec((tm, 1), lambda i: (i, 0)),
                  pl.BlockSpec((1, PAIR_W), lambda i: (0, 0)),
                  pl.BlockSpec((1, PAIR_W), lambda i: (0, 0))],
        out_specs=[pl.BlockSpec((tm, PAIR_W), lambda i: (i, 0))] * 2,
        out_shape=[jax.ShapeDtypeStruct((s, PAIR_W), F32)] * 2,
        compiler_params=_cparams(("parallel",)),
        name="rope_tables",
    )(pos_col, inv, sgn)


def _gates_body(z_ref, bias_ref, g_ref, mu_ref, u_ref, mprev_ref, mulast_ref, urow_ref,
                carry_ref, mcarry_ref):
    tm = z_ref.shape[0]

    @pl.when(pl.program_id(0) == 0)
    def _():
        carry_ref[...] = jnp.zeros_like(carry_ref)
        mcarry_ref[...] = jnp.zeros_like(mcarry_ref)

    v = z_ref[...] + bias_ref[...]
    ls = jnp.minimum(v, 0.0) - jnp.log1p(jnp.exp(-jnp.abs(v)))
    r = lax.broadcasted_iota(jnp.int32, (tm, tm), 0)
    c = lax.broadcasted_iota(jnp.int32, (tm, tm), 1)
    tri = jnp.where(c <= r, 1.0, 0.0).astype(BF16)
    tri_chunk = jnp.where((c <= r) & ((r // CHUNK) == (c // CHUNK)), 1.0, 0.0).astype(BF16)
    run = _dot_exact_lhs(tri, ls) + carry_ref[...]
    loc = _dot_exact_lhs(tri_chunk, ls)
    lane = lax.broadcasted_iota(jnp.int32, (1, 128), 1)
    out = jnp.where(lane < G_ML_I, run, jnp.where(lane < G_ML_F, v, loc))
    carry_ref[...] = run[tm - 1:tm, :]
    g_ref[...] = out

    ml_lane = (lane >= G_ML_F) & (lane < G_ML_F + N_HEADS)
    b = jnp.where(ml_lane, loc, 0.0)
    u = jnp.where(ml_lane, pltpu.roll(v, G_ML_F - G_ML_I, 1) - loc, 0.0)
    row_in_chunk = lax.broadcasted_iota(jnp.int32, (tm, 128), 0) % CHUNK
    run_max = u
    shift = 1
    while shift < CHUNK:
        run_max = jnp.where(row_in_chunk >= shift,
                            jnp.maximum(run_max, pltpu.roll(run_max, shift, 0)), run_max)
        shift *= 2
    m = mcarry_ref[...]
    m_prev_rows, mu_last_rows = [], []
    for ci in range(tm // CHUNK):
        last = ci * CHUNK + CHUNK - 1
        mu_last = jnp.maximum(m, run_max[last:last + 1, :])
        m_prev_rows.append(jnp.broadcast_to(m, (CHUNK, 128)))
        mu_last_rows.append(jnp.broadcast_to(mu_last, (CHUNK, 128)))
        m = b[last:last + 1, :] + mu_last
    mcarry_ref[...] = m
    m_prev = jnp.concatenate(m_prev_rows, axis=0)
    mu_ref[...] = jnp.maximum(m_prev, run_max)
    u_ref[...] = u
    mprev_ref[...] = m_prev
    mulast_ref[...] = jnp.concatenate(mu_last_rows, axis=0)
    u_t = u.T
    for ci in range(tm // CHUNK):
        cols = slice(ci * CHUNK, (ci + 1) * CHUNK)
        for p in range(N_PAIRS):
            h = G_ML_F + 2 * p
            urow_ref[ci, p:p + 1, :] = jnp.concatenate([u_t[h:h + 1, cols], u_t[h + 1:h + 2, cols]], axis=1)


def _gates(z, bias, tm):
    s = z.shape[0]
    col = pl.BlockSpec((tm, 128), lambda i: (i, 0))
    col_shape = jax.ShapeDtypeStruct((s, 128), F32)
    return pl.pallas_call(
        _gates_body,
        grid=(s // tm,),
        in_specs=[pl.BlockSpec((tm, 128), lambda i: (i, C_GATE_128)),
                  pl.BlockSpec((1, 128), lambda i: (0, 0))],
        out_specs=[col, col, col, col, col,
                   pl.BlockSpec((tm // CHUNK, N_PAIRS, PAIR_W), lambda i: (i, 0, 0))],
        out_shape=[col_shape] * 5 + [jax.ShapeDtypeStruct((s // CHUNK, N_PAIRS, PAIR_W), F32)],
        scratch_shapes=[pltpu.VMEM((1, 128), F32), pltpu.VMEM((1, 128), F32)],
        compiler_params=_cparams(("arbitrary",)),
        name="gates",
    )(z, bias)


FOX_SLOT = 128
FOX_BIAS_LANE = HEAD_DIM
LOG2E = 1.4426950408889634
FOX_MAX_STATIC_BOUND = 30.0


def _fox_prep_body(zq_ref, zk_ref, zv_ref, g_ref, shift_ref, qg_ref, kg_ref, spread_ref, place_ref,
                   q1_ref, k1_ref, q_ref, k_ref, v_ref):
    seg = _seg_matrix(MIX_W)
    q = zq_ref[...]
    k = zk_ref[...]
    q_ms = _seg_sum(q * q, seg) * (1.0 / HEAD_DIM)
    k_ms = _seg_sum(k * k, seg) * (1.0 / HEAD_DIM)
    qn = (q * lax.rsqrt(q_ms + NORM_EPS) * qg_ref[...] * (HEAD_DIM ** -0.5 * LOG2E)).astype(BF16)
    kn = (k * lax.rsqrt(k_ms + NORM_EPS) * kg_ref[...]).astype(BF16)
    spread = spread_ref[...]
    f2 = g_ref[...] * LOG2E
    f_q = sum(_dot(part, place_ref[n]) for n, part in enumerate(_split3(f2 - shift_ref[...])))
    f_k = sum(_dot(part, place_ref[3 + n]) for n, part in enumerate(_split3(f2)))
    q_ref[...] = (_dot(qn, spread) + f_q + q1_ref[...]).astype(BF16)
    k_ref[...] = (_dot(kn, spread) - f_k + k1_ref[...]).astype(BF16)
    v_ref[...] = zv_ref[...].astype(BF16)


def _fox_layout_consts():
    spread = np.zeros((MIX_W, N_HEADS * FOX_SLOT), np.float32)
    place = np.zeros((6, 128, N_HEADS * FOX_SLOT), np.float32)
    q_ones = np.zeros((1, N_HEADS * FOX_SLOT), np.float32)
    k_ones = np.zeros((1, N_HEADS * FOX_SLOT), np.float32)
    for h in range(N_HEADS):
        for dd in range(HEAD_DIM):
            spread[h * HEAD_DIM + dd, h * FOX_SLOT + dd] = 1.0
        for n in range(3):
            place[n, G_FOX_F + h, h * FOX_SLOT + FOX_BIAS_LANE + n] = 1.0
            place[3 + n, G_FOX_F + h, h * FOX_SLOT + FOX_BIAS_LANE + 3 + n] = 1.0
            k_ones[0, h * FOX_SLOT + FOX_BIAS_LANE + n] = 1.0
            q_ones[0, h * FOX_SLOT + FOX_BIAS_LANE + 3 + n] = 1.0
    return (jnp.asarray(spread, BF16), jnp.asarray(place, BF16), jnp.asarray(q_ones), jnp.asarray(k_ones))


def _fox_prep(z, g, shift, q_gain, k_gain, tm):
    s = z.shape[0]
    wide = N_HEADS * FOX_SLOT
    spread, place, q_ones, k_ones = _fox_layout_consts()
    zspec = lambda c: pl.BlockSpec((tm, MIX_W), lambda i: (i, c))
    return pl.pallas_call(
        _fox_prep_body,
        grid=(s // tm,),
        in_specs=[zspec(C_FQ), zspec(C_FK), zspec(C_FV),
                  pl.BlockSpec((tm, 128), lambda i: (i, 0)),
                  pl.BlockSpec((1, 128), lambda i: (0, 0)),
                  pl.BlockSpec((1, MIX_W), lambda i: (0, 0)),
                  pl.BlockSpec((1, MIX_W), lambda i: (0, 0)),
                  pl.BlockSpec((MIX_W, wide), lambda i: (0, 0)),
                  pl.BlockSpec((6, 128, wide), lambda i: (0, 0, 0)),
                  pl.BlockSpec((1, wide), lambda i: (0, 0)),
                  pl.BlockSpec((1, wide), lambda i: (0, 0))],
        out_specs=[pl.BlockSpec((tm, wide), lambda i: (i, 0)),
                   pl.BlockSpec((tm, wide), lambda i: (i, 0)),
                   pl.BlockSpec((tm, MIX_W), lambda i: (i, 0))],
        out_shape=[jax.ShapeDtypeStruct((s, wide), BF16),
                   jax.ShapeDtypeStruct((s, wide), BF16),
                   jax.ShapeDtypeStruct((s, MIX_W), BF16)],
        compiler_params=_cparams(("parallel",)),
        name="fox_prep",
    )(z, z, z, g, shift, q_gain, k_gain, spread, place, q_ones, k_ones)


def _fox_body(it_ref, jt_ref, q_ref, k_ref, v_ref, og_ref, o_ref, m_ref, l_ref, acc_ref):
    t = pl.program_id(1)
    i = it_ref[t]
    j = jt_ref[t]
    tq, tk = q_ref.shape[0], k_ref.shape[0]
    m0 = _lane_is_head0((1, PAIR_W))

    @pl.when(j == 0)
    def _():
        m_ref[...] = jnp.full_like(m_ref, NEG_BIG)
        l_ref[...] = jnp.zeros_like(l_ref)
        acc_ref[...] = jnp.zeros_like(acc_ref)

    def step(on_diagonal):
        v = v_ref[...]
        pvs, alphas = [], []
        for e in range(2):
            slot = slice(e * FOX_SLOT, (e + 1) * FOX_SLOT)
            s = _dot_nt(q_ref[:, slot], k_ref[:, slot])
            if on_diagonal:
                row = lax.broadcasted_iota(jnp.int32, (tq, tk), 0)
                col = lax.broadcasted_iota(jnp.int32, (tq, tk), 1)
                s = jnp.where(col <= row, s, NEG_BIG)
            m_prev = m_ref[e]
            m_new = jnp.maximum(m_prev, jnp.max(s, axis=1, keepdims=True))
            alpha = jnp.exp2(m_prev - m_new)
            pr = jnp.exp2(s - m_new)
            l_ref[e] = alpha * l_ref[e] + jnp.sum(pr, axis=1, keepdims=True)
            m_ref[e] = m_new
            pvs.append(_dot(pr.astype(BF16), v))
            alphas.append(alpha)
        acc_ref[...] = (jnp.where(m0, alphas[0], alphas[1]) * acc_ref[...]
                        + jnp.where(m0, pvs[0], pvs[1]))

    @pl.when(j < i)
    def _():
        step(False)

    @pl.when(j == i)
    def _():
        step(True)
        l_pair = jnp.where(m0, l_ref[0], l_ref[1])
        o_ref[...] = (acc_ref[...] / l_pair * _sigmoid(og_ref[...])).astype(o_ref.dtype)


def _fox_bounded_body(it_ref, jt_ref, q_ref, k_ref, v_ref, og_ref, o_ref, acc_ref):
    t = pl.program_id(1)
    i = it_ref[t]
    j = jt_ref[t]
    tq, tk = q_ref.shape[0], k_ref.shape[0]

    @pl.when(j == 0)
    def _():
        acc_ref[...] = jnp.zeros_like(acc_ref)

    n_pairs = v_ref.shape[1] // PAIR_W

    def step(on_diagonal):
        for p in range(n_pairs):
            v = v_ref[:, p * PAIR_W:(p + 1) * PAIR_W]
            v_ones = jnp.concatenate([v, jnp.ones_like(v)], axis=1)
            for e in range(2):
                h = 2 * p + e
                slot = slice(h * FOX_SLOT, (h + 1) * FOX_SLOT)
                pr = jnp.exp2(_dot_nt(q_ref[:, slot], k_ref[:, slot]))
                if on_diagonal:
                    row = lax.broadcasted_iota(jnp.int32, (tq, tk), 0)
                    col = lax.broadcasted_iota(jnp.int32, (tq, tk), 1)
                    pr = jnp.where(col <= row, pr, 0.0)
                acc_ref[h] += _dot(pr.astype(BF16), v_ones)

    @pl.when(j < i)
    def _():
        step(False)

    @pl.when(j == i)
    def _():
        step(True)
        m0 = _lane_is_head0((1, PAIR_W))
        for p in range(n_pairs):
            cols = slice(p * PAIR_W, (p + 1) * PAIR_W)
            num = jnp.where(m0, acc_ref[2 * p, :, :PAIR_W], acc_ref[2 * p + 1, :, :PAIR_W])
            den = jnp.where(m0, acc_ref[2 * p, :, PAIR_W:], acc_ref[2 * p + 1, :, PAIR_W:])
            o_ref[:, cols] = (num / den * _sigmoid(og_ref[:, cols])).astype(o_ref.dtype)


def _fox_attention(q, k, v, z, tq, bounded):
    s = v.shape[0]
    nq = s // tq
    ii, jj = np.tril_indices(nq)
    it = jnp.asarray(ii, jnp.int32)
    jt = jnp.asarray(jj, jnp.int32)
    if bounded:
        body = _fox_bounded_body
        pairs_per_step = N_PAIRS
        scratch = [pltpu.VMEM((N_HEADS, tq, 2 * PAIR_W), F32)]
    else:
        body = _fox_body
        pairs_per_step = 1
        scratch = [pltpu.VMEM((2, tq, 1), F32), pltpu.VMEM((2, tq, 1), F32),
                   pltpu.VMEM((tq, PAIR_W), F32)]
    wide = pairs_per_step * 2 * FOX_SLOT
    narrow = pairs_per_step * PAIR_W
    fo_blk = C_FO * (MIX_W // narrow)
    grid_spec = pltpu.PrefetchScalarGridSpec(
        num_scalar_prefetch=2,
        grid=(N_PAIRS // pairs_per_step, len(ii)),
        in_specs=[
            pl.BlockSpec((tq, wide), lambda p, t, it, jt: (it[t], p)),
            pl.BlockSpec((tq, wide), lambda p, t, it, jt: (jt[t], p)),
            pl.BlockSpec((tq, narrow), lambda p, t, it, jt: (jt[t], p)),
            pl.BlockSpec((tq, narrow), lambda p, t, it, jt: (it[t], fo_blk + p)),
        ],
        out_specs=pl.BlockSpec((tq, narrow), lambda p, t, it, jt: (it[t], p)),
        scratch_shapes=scratch,
    )
    return pl.pallas_call(
        body,
        grid_spec=grid_spec,
        out_shape=jax.ShapeDtypeStruct((s, MIX_W), BF16),
        compiler_params=_cparams(("parallel", "arbitrary")),
        name="fox_attention_bounded" if bounded else "fox_attention",
    )(it, jt, q, k, v, z)


def _retention_body(zq_ref, zk_ref, zv_ref, zg_ref, cos_ref, sin_ref, intra_ref, qd_ref, kd_ref,
                    cd_ref, gain_ref, o_ref, st_ref, q_scr, k_scr, y_scr):
    tm = zq_ref.shape[0]

    @pl.when(pl.program_id(0) == 0)
    def _():
        st_ref[...] = jnp.zeros_like(st_ref)

    lane = lax.broadcasted_iota(jnp.int32, (1, PAIR_W), 1)
    m0 = lane < HEAD_DIM
    first_half = (lane & (HEAD_DIM // 2)) == 0
    cos = cos_ref[...]
    sin = sin_ref[...]
    for p in range(N_PAIRS):
        cs = slice(p * PAIR_W, (p + 1) * PAIR_W)
        for src, dst, scale in ((zq_ref, q_scr, 1.0), (zk_ref, k_scr, HEAD_DIM ** -0.5)):
            x = src[:, cs]
            swapped = jnp.where(first_half, pltpu.roll(x, PAIR_W - HEAD_DIM // 2, 1),
                                pltpu.roll(x, HEAD_DIM // 2, 1))
            dst[:, cs] = (x * cos + swapped * sin) * scale
    bd, _, _ = _block_diag_mask()

    pairs = range(N_PAIRS)
    pair_cols = [slice(p * PAIR_W, (p + 1) * PAIR_W) for p in pairs]

    def chunk(c, carry):
        rows = pl.ds(pl.multiple_of(c * CHUNK, CHUNK), CHUNK)
        q = [q_scr[rows, cols] for cols in pair_cols]
        k = [k_scr[rows, cols] for cols in pair_cols]
        vb = [zv_ref[rows, cols].astype(BF16) for cols in pair_cols]
        st = [st_ref[p] for p in pairs]
        sc = [_dot_nt(_stack_heads(q[p], m0).astype(BF16), k[p].astype(BF16)) * intra_ref[p]
              for p in pairs]
        o_inter = [_dot((q[p] * qd_ref[p]).astype(BF16), st[p].astype(BF16)) for p in pairs]
        upd = [_dot_tn((k[p] * kd_ref[p]).astype(BF16), vb[p]) for p in pairs]
        o_intra = [_unstack_rows(_dot(sc[p].astype(BF16), vb[p]), m0) for p in pairs]
        for p in pairs:
            y_scr[rows, pair_cols[p]] = o_intra[p] + o_inter[p]
            st_ref[p] = st[p] * cd_ref[p] + jnp.where(bd, upd[p], 0.0)
        return carry

    lax.fori_loop(0, tm // CHUNK, chunk, 0)
    y = y_scr[...]
    ms = _seg_sum(y * y, _seg_matrix(MIX_W)) * (1.0 / HEAD_DIM)
    o_ref[...] = (y * lax.rsqrt(ms + NORM_EPS) * gain_ref[...] * _silu(zg_ref[...])).astype(o_ref.dtype)


def _retention_tables():
    hh = jnp.arange(N_HEADS, dtype=F32)
    log_gamma = jnp.log1p(-jnp.exp2(-5.0 - hh))
    idx = jnp.arange(CHUNK, dtype=F32)
    intra = jnp.exp(log_gamma[:, None, None] * jnp.abs(idx[:, None] - idx[None, :]))
    q_decay = jnp.exp(log_gamma[:, None] * (idx + 1.0))
    k_decay = jnp.exp(log_gamma[:, None] * (CHUNK - 1.0 - idx))
    chunk_decay = jnp.exp(log_gamma * CHUNK)
    intra_st = intra.reshape(N_PAIRS, 2 * CHUNK, CHUNK)
    to_pair = lambda t: jnp.repeat(t.reshape(N_PAIRS, 2, CHUNK).transpose(0, 2, 1), HEAD_DIM, axis=2)
    cd = jnp.repeat(chunk_decay.reshape(N_PAIRS, 1, 2), HEAD_DIM, axis=2)
    return intra_st, to_pair(q_decay), to_pair(k_decay), cd


def _retention(z, cos, sin, gain, tm):
    s = z.shape[0]
    intra_st, qd, kd, cd = _retention_tables()
    zspec = lambda c: pl.BlockSpec((tm, MIX_W), lambda i: (i, c))
    full = lambda a: pl.BlockSpec(a.shape, lambda i: (0,) * a.ndim)
    return pl.pallas_call(
        _retention_body,
        grid=(s // tm,),
        in_specs=[zspec(C_TQ), zspec(C_TK), zspec(C_TV), zspec(C_TG),
                  pl.BlockSpec((tm, PAIR_W), lambda i: (i, 0)),
                  pl.BlockSpec((tm, PAIR_W), lambda i: (i, 0)),
                  full(intra_st), full(qd), full(kd), full(cd),
                  pl.BlockSpec((1, MIX_W), lambda i: (0, 0))],
        out_specs=pl.BlockSpec((tm, MIX_W), lambda i: (i, 0)),
        out_shape=jax.ShapeDtypeStruct((s, MIX_W), BF16),
        scratch_shapes=[pltpu.VMEM((N_PAIRS, PAIR_W, PAIR_W), F32),
                        pltpu.VMEM((tm, MIX_W), F32), pltpu.VMEM((tm, MIX_W), F32),
                        pltpu.VMEM((tm, MIX_W), F32)],
        compiler_params=_cparams(("arbitrary",)),
        name="retention",
    )(z, z, z, z, cos, sin, intra_st, qd, kd, cd, gain)


def _load_with_halo(ext_ref, cur_ref, halo_ref, first):
    halo = halo_ref[...]
    ext_ref[0:HALO, :] = jnp.where(first, jnp.zeros_like(halo), halo)
    ext_ref[HALO:, :] = cur_ref[...]


def _mlstm_body(zq_ref, zk_ref, hq_ref, hk_ref, zv_ref, zo_ref, g_ref, mu_ref, u_ref, mprev_ref,
                mulast_ref, urow_ref, spread_ref, cw_ref, cb_ref, gain_ref, o_ref, c_ref, n_ref,
                ext_scr, q_scr, k_scr, y_scr, mu_scr, inter_scr, floor_scr, w_scr, dec_scr):
    tm = zq_ref.shape[0]
    first = pl.program_id(0) == 0

    @pl.when(first)
    def _():
        c_ref[...] = jnp.zeros_like(c_ref)
        n_ref[...] = jnp.zeros_like(n_ref)

    lane_g = lax.broadcasted_iota(jnp.int32, (1, 128), 1)
    ml_lane = (lane_g >= G_ML_F) & (lane_g < G_ML_F + N_HEADS)
    spread = spread_ref[...]
    to_heads = lambda t: _dot_exact_rhs(jnp.where(ml_lane, t, 0.0), spread)
    mu = mu_ref[...]
    m_prev = mprev_ref[...]
    mu_last = mulast_ref[...]
    mu_scr[...] = to_heads(mu)
    inter_scr[...] = to_heads(jnp.exp(m_prev - mu))
    floor_scr[...] = to_heads(jnp.exp(-(g_ref[...] + mu)))
    w_scr[...] = to_heads(jnp.exp(u_ref[...] - mu_last))
    dec_scr[...] = to_heads(jnp.exp(m_prev - mu_last))

    for part, (cur, halo, dst, scale) in enumerate(((zq_ref, hq_ref, q_scr, 1.0),
                                                    (zk_ref, hk_ref, k_scr, HEAD_DIM ** -0.5))):
        _load_with_halo(ext_scr, cur, halo, first)
        cols = slice(part * MIX_W, (part + 1) * MIX_W)
        acc = cb_ref[:, cols]
        for tap in range(CONV_W):
            off = HALO - (CONV_W - 1) + tap
            acc = acc + cw_ref[tap:tap + 1, cols] * ext_scr[off:off + tm, :]
        dst[...] = _silu(acc) * scale

    lane = lax.broadcasted_iota(jnp.int32, (1, PAIR_W), 1)
    m0 = lane < HEAD_DIM
    bd, _, _ = _block_diag_mask()
    bd_ones = jnp.where(bd, 1.0, 0.0).astype(BF16)
    causal = (lax.broadcasted_iota(jnp.int32, (CHUNK, PAIR_W), 1) % HEAD_DIM
              <= lax.broadcasted_iota(jnp.int32, (CHUNK, PAIR_W), 0))

    pairs = range(N_PAIRS)
    pair_cols = [slice(p * PAIR_W, (p + 1) * PAIR_W) for p in pairs]

    def chunk(c, carry):
        rows = pl.ds(pl.multiple_of(c * CHUNK, CHUNK), CHUNK)
        first8 = pl.ds(pl.multiple_of(c * CHUNK, CHUNK), 8)
        u_rows = urow_ref[c]
        q = [q_scr[rows, cols] for cols in pair_cols]
        k = [k_scr[rows, cols] for cols in pair_cols]
        v = [zv_ref[rows, cols] for cols in pair_cols]
        cst = [c_ref[p] for p in pairs]
        nst = [n_ref[p] for p in pairs]
        qb = [q[p].astype(BF16) for p in pairs]
        sc = [_dot_nt(qb[p], _stack_heads(k[p], m0).astype(BF16)) for p in pairs]
        q_c = [_dot(qb[p], cst[p].astype(BF16)) for p in pairs]
        q_n = [_seg_sum(q[p] * nst[p], bd_ones) for p in pairs]
        d_mat = [jnp.exp(u_rows[p:p + 1, :] - mu_scr[rows, pair_cols[p]]) for p in pairs]
        pr = [jnp.where(causal, sc[p] * d_mat[p], 0.0).astype(BF16) for p in pairs]
        v_ones = [jnp.concatenate([_stack_heads(v[p], m0).astype(BF16), bd_ones], axis=1) for p in pairs]
        nd = [_dot(pr[p], v_ones[p]) for p in pairs]
        kw = [k[p] * w_scr[rows, pair_cols[p]] for p in pairs]
        upd = [_dot_tn(kw[p].astype(BF16), v[p].astype(BF16)) for p in pairs]
        for p in pairs:
            inter = inter_scr[rows, pair_cols[p]]
            num = nd[p][:, :PAIR_W] + inter * q_c[p]
            den = nd[p][:, PAIR_W:] + inter * q_n[p]
            y_scr[rows, pair_cols[p]] = num / jnp.maximum(jnp.abs(den), floor_scr[rows, pair_cols[p]])
            dec = dec_scr[first8, pair_cols[p]][0:1]
            c_ref[p] = dec * cst[p] + jnp.where(bd, upd[p], 0.0)
            n_ref[p] = dec * nst[p] + jnp.sum(kw[p], axis=0, keepdims=True)
        return carry

    lax.fori_loop(0, tm // CHUNK, chunk, 0)
    y = y_scr[...] * _sigmoid(zo_ref[...])
    ms = _seg_sum(y * y, _seg_matrix(MIX_W)) * (1.0 / HEAD_DIM)
    o_ref[...] = (y * lax.rsqrt(ms + NORM_EPS) * gain_ref[...]).astype(o_ref.dtype)


def _halo_spec(tm, width, c):
    step = tm // HALO
    return pl.BlockSpec((HALO, width), lambda i: (jnp.maximum(i * step - 1, 0), c))


def _mlstm(z, gate_cols, urow, conv_w, conv_b, gain, tm):
    s = z.shape[0]
    spread = np.zeros((128, MIX_W), np.float32)
    for h in range(N_HEADS):
        spread[G_ML_F + h, h * HEAD_DIM:(h + 1) * HEAD_DIM] = 1.0
    zspec = lambda c: pl.BlockSpec((tm, MIX_W), lambda i: (i, c))
    col = pl.BlockSpec((tm, 128), lambda i: (i, 0))
    big = pltpu.VMEM((tm, MIX_W), F32)
    return pl.pallas_call(
        _mlstm_body,
        grid=(s // tm,),
        in_specs=[zspec(C_MQ), zspec(C_MK), _halo_spec(tm, MIX_W, C_MQ), _halo_spec(tm, MIX_W, C_MK),
                  zspec(C_MV), zspec(C_MO),
                  col, col, col, col, col,
                  pl.BlockSpec((tm // CHUNK, N_PAIRS, PAIR_W), lambda i: (i, 0, 0)),
                  pl.BlockSpec((128, MIX_W), lambda i: (0, 0)),
                  pl.BlockSpec((CONV_W, 2 * MIX_W), lambda i: (0, 0)),
                  pl.BlockSpec((1, 2 * MIX_W), lambda i: (0, 0)),
                  pl.BlockSpec((1, MIX_W), lambda i: (0, 0))],
        out_specs=pl.BlockSpec((tm, MIX_W), lambda i: (i, 0)),
        out_shape=jax.ShapeDtypeStruct((s, MIX_W), BF16),
        scratch_shapes=[pltpu.VMEM((N_PAIRS, PAIR_W, PAIR_W), F32),
                        pltpu.VMEM((N_PAIRS, 1, PAIR_W), F32),
                        pltpu.VMEM((tm + HALO, MIX_W), F32),
                        big, big, big, big, big, big, big, big],
        compiler_params=_cparams(("arbitrary",)),
        name="mlstm",
    )(z, z, z, z, z, z, *gate_cols, urow, jnp.asarray(spread, BF16), conv_w, conv_b, gain)


def _rwkv_body(zr_ref, zk_ref, zv_ref, zl_ref, hr_ref, hk_ref, hv_ref, hl_ref, mu_ref, mul_ref,
               wl_ref, w0_ref, a0_ref, kk_ref, ka_ref, rk_ref, gnw_ref, gnb_ref, o_ref,
               h_ref, ext_scr, extl_scr, rt_scr, kt_scr, kb_scr, bb_scr, kh_scr, bh_scr,
               gam_scr, v_scr, y_scr):
    tm = zr_ref.shape[0]
    first = pl.program_id(0) == 0

    @pl.when(first)
    def _():
        h_ref[...] = jnp.zeros_like(h_ref)

    def shifted(ext, cur_ref, halo_ref, mu):
        _load_with_halo(ext, cur_ref, halo_ref, first)
        cur = ext[HALO:HALO + tm, :]
        prev = ext[HALO - 1:HALO - 1 + tm, :]
        return cur + mu * (prev - cur)

    r = shifted(ext_scr, zr_ref, hr_ref, mu_ref[:, 0:MIX_W])
    k = shifted(ext_scr, zk_ref, hk_ref, mu_ref[:, MIX_W:2 * MIX_W])
    v = shifted(ext_scr, zv_ref, hv_ref, mu_ref[:, 2 * MIX_W:3 * MIX_W])
    lora_in = shifted(extl_scr, zl_ref, hl_ref, mul_ref[...])
    lane_l = lax.broadcasted_iota(jnp.int32, (1, LORA_W), 1)
    act = jnp.where(lane_l < 64, jnp.tanh(lora_in), jnp.where(lane_l < 128, lora_in, _sigmoid(lora_in)))
    lora = _dot(act.astype(BF16), wl_ref[...])
    logw = -_sigmoid(w0_ref[...] + lora[:, 0:MIX_W]) * math.exp(-0.5)
    a = _sigmoid(a0_ref[...] + lora[:, MIX_W:2 * MIX_W])
    gate = lora[:, 2 * MIX_W:3 * MIX_W]

    seg = _seg_matrix(MIX_W)
    kk = k * kk_ref[...]
    kk = kk * lax.rsqrt(jnp.maximum(_seg_sum(kk * kk, seg), 1e-12))
    k2 = k * (1.0 + (a - 1.0) * ka_ref[...])
    b = kk * a
    bonus = _seg_sum(r * k2 * rk_ref[...], seg) * v

    ri = lax.broadcasted_iota(jnp.int32, (tm, tm), 0)
    ci = lax.broadcasted_iota(jnp.int32, (tm, tm), 1)
    same = (ri // CHUNK) == (ci // CHUNK)
    lower = jnp.where(same & (ci <= ri), 1.0, 0.0).astype(BF16)
    upper = jnp.where(same & (ci > ri), 1.0, 0.0).astype(BF16)
    cl = _dot_exact_lhs(lower, logw)
    cs = _dot_exact_lhs(upper, logw)
    e_cl = jnp.exp(cl)
    e_ncl = jnp.exp(-cl)
    e_cs = jnp.exp(cs)
    rt_scr[...] = r * e_cl
    kt_scr[...] = kk * jnp.exp(cl - logw)
    kb_scr[...] = k2 * e_ncl
    bb_scr[...] = b * e_ncl
    kh_scr[...] = k2 * e_cs
    bh_scr[...] = b * e_cs
    gam_scr[...] = jnp.exp(cl + cs)
    v_scr[...] = v

    lane = lax.broadcasted_iota(jnp.int32, (1, PAIR_W), 1)
    m0 = lane < HEAD_DIM
    bd, rr, cc = _block_diag_mask()
    strict = bd & (cc < rr)
    incl = bd & (cc <= rr)
    eye = bd & (cc == rr)
    eye_f = jnp.where(eye, 1.0, 0.0)

    pairs = range(N_PAIRS)
    pair_cols = [slice(p * PAIR_W, (p + 1) * PAIR_W) for p in pairs]

    chunks_per_trip = 2
    units = [(cc, p) for cc in range(chunks_per_trip) for p in pairs]

    def trip(c, carry):
        base = pl.multiple_of(c * (chunks_per_trip * CHUNK), chunks_per_trip * CHUNK)
        rows = [pl.ds(base + cc * CHUNK, CHUNK) for cc in range(chunks_per_trip)]
        first8 = [pl.ds(base + cc * CHUNK, 8) for cc in range(chunks_per_trip)]
        stacked = lambda scr: [_stack_heads(scr[rows[cc], pair_cols[p]], m0) for cc, p in units]
        rt_st = stacked(rt_scr)
        kt_st = stacked(kt_scr)
        v_st = [t.astype(BF16) for t in stacked(v_scr)]
        kh_st = [t.astype(BF16) for t in stacked(kh_scr)]
        bh_st = [t.astype(BF16) for t in stacked(bh_scr)]
        twice = lambda scr: [jnp.concatenate([scr[rows[cc], pair_cols[p]].astype(BF16)] * 2, axis=0)
                             for cc, p in units]
        bb2 = twice(bb_scr)
        kb2 = twice(kb_scr)
        gam = [gam_scr[first8[cc], pair_cols[p]][0:1] for cc, p in units]
        us = range(len(units))
        lhs = [jnp.concatenate([kt_st[u], rt_st[u]], axis=0).astype(BF16) for u in us]
        g_b = [_dot_nt(lhs[u], bb2[u]) for u in us]
        g_k = [_dot_nt(lhs[u], kb2[u]) for u in us]
        x = [jnp.where(strict, -g_b[u][:PAIR_W], 0.0) for u in us]
        q_bd = [jnp.where(incl, g_b[u][PAIR_W:], 0.0).astype(BF16) for u in us]
        b_bd = [jnp.where(strict, g_k[u][:PAIR_W], 0.0).astype(BF16) for u in us]
        p_bd = [jnp.where(incl, g_k[u][PAIR_W:], 0.0).astype(BF16) for u in us]
        bv = [_dot(b_bd[u], v_st[u]).astype(BF16) for u in us]
        pv = [_dot(p_bd[u], v_st[u]) for u in us]
        khv = [_dot_tn(kh_st[u], v_st[u]) for u in us]
        t_inv = [eye_f + x[u] for u in us]
        for _ in range(5):
            xb = [x[u].astype(BF16) for u in us]
            x = [_dot(xb[u], xb[u]) for u in us]
            t_inv = [t_inv[u] + _dot(t_inv[u].astype(BF16), x[u].astype(BF16)) for u in us]
        tb = [t_inv[u].astype(BF16) for u in us]
        kt2 = [_dot(tb[u], kt_st[u].astype(BF16)).astype(BF16) for u in us]
        w1 = [_dot(tb[u], bv[u]).astype(BF16) for u in us]
        y1 = [(rt_st[u] - _dot(q_bd[u], kt2[u])).astype(BF16) for u in us]
        y0 = [pv[u] - _dot(q_bd[u], w1[u]) for u in us]
        m_mat = [(jnp.where(eye, gam[u], 0.0) - _dot_tn(bh_st[u], kt2[u])).astype(BF16) for u in us]
        n_mat = [khv[u] - _dot_tn(bh_st[u], w1[u]) for u in us]
        h = [h_ref[p] for p in pairs]
        for u, (cc, p) in enumerate(units):
            hb = h[p].astype(BF16)
            y_st = _dot(y1[u], hb) + y0[u]
            y_scr[rows[cc], pair_cols[p]] = y_st[:CHUNK] + y_st[CHUNK:]
            h[p] = _dot(m_mat[u], hb) + n_mat[u]
        for p in pairs:
            h_ref[p] = h[p]
        return carry

    lax.fori_loop(0, tm // (chunks_per_trip * CHUNK), trip, 0)

    y = y_scr[...]
    mean = _seg_sum(y, seg) * (1.0 / HEAD_DIM)
    yc = y - mean
    var = _seg_sum(yc * yc, seg) * (1.0 / HEAD_DIM)
    yn = yc * lax.rsqrt(var + RWKV_GN_EPS) * gnw_ref[...] + gnb_ref[...]
    o_ref[...] = ((yn + bonus) * gate).astype(o_ref.dtype)


def _rwkv(z, mu, mu_l, w_lora, w0, a0, k_k, k_a, r_k, gn_w, gn_b, tm):
    s = z.shape[0]
    zspec = lambda c: pl.BlockSpec((tm, MIX_W), lambda i: (i, c))
    row = lambda w: pl.BlockSpec((1, w), lambda i: (0, 0))
    big = pltpu.VMEM((tm, MIX_W), F32)
    return pl.pallas_call(
        _rwkv_body,
        grid=(s // tm,),
        in_specs=[zspec(C_RR), zspec(C_RK), zspec(C_RV),
                  pl.BlockSpec((tm, LORA_W), lambda i: (i, C_RL_256)),
                  _halo_spec(tm, MIX_W, C_RR), _halo_spec(tm, MIX_W, C_RK), _halo_spec(tm, MIX_W, C_RV),
                  _halo_spec(tm, LORA_W, C_RL_256),
                  row(3 * MIX_W), row(LORA_W),
                  pl.BlockSpec((LORA_W, 3 * MIX_W), lambda i: (0, 0)),
                  row(MIX_W), row(MIX_W), row(MIX_W), row(MIX_W), row(MIX_W), row(MIX_W), row(MIX_W)],
        out_specs=pl.BlockSpec((tm, MIX_W), lambda i: (i, 0)),
        out_shape=jax.ShapeDtypeStruct((s, MIX_W), BF16),
        scratch_shapes=[pltpu.VMEM((N_PAIRS, PAIR_W, PAIR_W), F32),
                        pltpu.VMEM((tm + HALO, MIX_W), F32), pltpu.VMEM((tm + HALO, LORA_W), F32),
                        big, big, big, big, big, big, big, big, big],
        compiler_params=_cparams(("arbitrary",)),
        name="rwkv7",
    )(z, z, z, z, z, z, z, z, mu, mu_l, w_lora, w0, a0, k_k, k_a, r_k, gn_w, gn_b)


def _w_in_column_map():
    fox, rwkv, ret, ml = 0, 2056, 3848, 5896
    src = np.full((Z_COLS,), -1, np.int64)

    def put(dst, start, n):
        src[dst:dst + n] = np.arange(start, start + n)

    for blk, start in ((C_FQ, fox), (C_FK, fox + 512), (C_FV, fox + 1024), (C_FO, fox + 1536),
                       (C_RR, rwkv), (C_RK, rwkv + 512), (C_RV, rwkv + 1024),
                       (C_TQ, ret), (C_TK, ret + 512), (C_TV, ret + 1024), (C_TG, ret + 1536),
                       (C_MQ, ml), (C_MK, ml + 512), (C_MV, ml + 1024), (C_MO, ml + 1536)):
        put(blk * MIX_W, start, MIX_W)
    put(C_RL_256 * LORA_W, rwkv + 1536, LORA_W)
    gate0 = C_GATE_128 * 128
    put(gate0 + G_FOX_F, fox + 2048, N_HEADS)
    put(gate0 + G_ML_I, ml + 2048, N_HEADS)
    put(gate0 + G_ML_F, ml + 2056, N_HEADS)
    return src


def _w_in_runs():
    src = _w_in_column_map()
    runs, start = [], 0
    for pos in range(1, Z_COLS + 1):
        run_continues = pos < Z_COLS and (
            (src[pos] < 0 and src[pos - 1] < 0) or (src[pos - 1] >= 0 and src[pos] == src[pos - 1] + 1))
        if not run_continues:
            runs.append((start, int(src[start]), pos - start))
            start = pos
    return runs


def _w_in_body(w_ref, o_ref):
    for dst, src, n in _w_in_runs():
        if src < 0:
            o_ref[:, dst:dst + n] = jnp.zeros((o_ref.shape[0], n), BF16)
        else:
            o_ref[:, dst:dst + n] = w_ref[:, src:src + n].astype(BF16)


def _permute_w_in(w_in, layer, tr):
    _, d, d_in = w_in.shape
    return pl.pallas_call(
        _w_in_body,
        grid=(d // tr,),
        in_specs=[pl.BlockSpec((None, tr, d_in), lambda i: (layer, i, 0))],
        out_specs=pl.BlockSpec((tr, Z_COLS), lambda i: (i, 0)),
        out_shape=jax.ShapeDtypeStruct((d, Z_COLS), BF16),
        compiler_params=_cparams(("parallel",)),
        name="w_in_layout",
    )(w_in)


def _cast_body(w_ref, o_ref):
    o_ref[...] = w_ref[...].astype(BF16)


def _layer_bf16(w, layer):
    _, r, c = w.shape
    tr = r
    while tr * c > CAST_BLOCK_ELEMS and tr % 16 == 0:
        tr //= 2
    return pl.pallas_call(
        _cast_body,
        grid=(r // tr,),
        in_specs=[pl.BlockSpec((None, tr, c), lambda i: (layer, i, 0))],
        out_specs=pl.BlockSpec((tr, c), lambda i: (i, 0)),
        out_shape=jax.ShapeDtypeStruct((r, c), BF16),
        compiler_params=_cparams(("parallel",)),
        name="weight_bf16",
    )(w)


def kernel(x, positions, ffn1_norm, ffn1_w1, ffn1_w3, ffn1_w2, mix_norm, w_in, fox_f_bias, fox_q_gain, fox_k_gain, rwkv_shift_mu, rwkv_w0, rwkv_w2, rwkv_a0, rwkv_a2, rwkv_g2, rwkv_k_k, rwkv_k_a, rwkv_r_k, rwkv_gn_w, rwkv_gn_b, ret_gn_gain, mlstm_conv_w, mlstm_conv_b, mlstm_i_bias, mlstm_f_bias, mlstm_gn_gain, w_merge_gate, merge_gate_bias, w_branch, w_out, ffn2_norm, ffn2_w1, ffn2_w3, ffn2_w2):
    batch, seq, d = x.shape
    depth = w_in.shape[0]
    dff = ffn1_w1.shape[-1]
    pick = lambda full, want: want if full % want == 0 else full
    tm_dense = pick(seq, 512)
    tm_inproj = pick(seq, 1024)
    tf = pick(dff, 512)
    te = pick(d, 256)
    tn = pick(Z_COLS, 1024)
    tm_row = pick(seq, 512)
    tm_seq = pick(seq, 256)
    tq = pick(seq, 512)
    row = lambda t: t.reshape(1, -1).astype(F32)

    outs = []
    for bi in range(batch):
        xb = x[bi]
        pos_col = positions[bi].astype(F32).reshape(seq, 1)
        cos, sin = _rope_tables(pos_col, tm_row)
        for l in range(depth):
            xb = _ffn(xb, row(ffn1_norm[l]), _layer_bf16(ffn1_w1, l), _layer_bf16(ffn1_w3, l),
                      _layer_bf16(ffn1_w2, l), tm_dense, tf)
            z = _inproj(xb, row(mix_norm[l]), _permute_w_in(w_in, l, pick(d, 256)), tm_inproj, tn)

            gate_bias = jnp.zeros((128,), F32)
            gate_bias = gate_bias.at[G_FOX_F:G_FOX_F + N_HEADS].set(fox_f_bias[l])
            gate_bias = gate_bias.at[G_ML_I:G_ML_I + N_HEADS].set(mlstm_i_bias[l])
            gate_bias = gate_bias.at[G_ML_F:G_ML_F + N_HEADS].set(mlstm_f_bias[l])
            g, ml_mu, ml_u, ml_mprev, ml_mulast, ml_urow = _gates(z, gate_bias[None, :], tm_row)

            bound = (LOG2E * HEAD_DIM ** 0.5) * jnp.max(jnp.abs(fox_q_gain[l])) * jnp.max(jnp.abs(fox_k_gain[l]))
            bounded = bound <= FOX_MAX_STATIC_BOUND
            shift = jnp.where(bounded, bound, 0.0) * jnp.ones((1, 128), F32)
            fq, fk, fv = _fox_prep(z, g, shift, row(jnp.tile(fox_q_gain[l], N_HEADS)),
                                   row(jnp.tile(fox_k_gain[l], N_HEADS)), tm_row)
            y_fox = lax.cond(bounded,
                             lambda *a: _fox_attention(*a, tq, True),
                             lambda *a: _fox_attention(*a, tq, False), fq, fk, fv, z)

            mu = rwkv_shift_mu[l]
            w_lora = jnp.zeros((LORA_W, 3 * MIX_W), F32)
            w_lora = w_lora.at[0:64, 0:MIX_W].set(rwkv_w2[l])
            w_lora = w_lora.at[64:128, MIX_W:2 * MIX_W].set(rwkv_a2[l])
            w_lora = w_lora.at[128:256, 2 * MIX_W:3 * MIX_W].set(rwkv_g2[l])
            y_rwkv = _rwkv(z, row(mu[:3 * MIX_W]), row(mu[3 * MIX_W:]), w_lora.astype(BF16),
                           row(rwkv_w0[l]), row(rwkv_a0[l]), row(rwkv_k_k[l]), row(rwkv_k_a[l]),
                           row(rwkv_r_k[l]), row(rwkv_gn_w[l]), row(rwkv_gn_b[l]), tm_seq)

            y_ret = _retention(z, cos, sin, row(ret_gn_gain[l]), tm_seq)

            y_m = _mlstm(z, (g, ml_mu, ml_u, ml_mprev, ml_mulast), ml_urow,
                         mlstm_conv_w[l].astype(F32), row(mlstm_conv_b[l]), row(mlstm_gn_gain[l]), tm_seq)

            xb = _merge(xb, row(mix_norm[l]), (y_fox, y_rwkv, y_ret, y_m),
                        _layer_bf16(w_merge_gate.reshape(depth, d, -1), l), merge_gate_bias[l].astype(F32),
                        _layer_bf16(w_branch.reshape(depth, -1, d), l).reshape(w_branch.shape[1:]),
                        _layer_bf16(w_out, l), tm_dense, te)
            xb = _ffn(xb, row(ffn2_norm[l]), _layer_bf16(ffn2_w1, l), _layer_bf16(ffn2_w3, l),
                      _layer_bf16(ffn2_w2, l), tm_dense, tf)
        outs.append(xb)
    return jnp.stack(outs, axis=0)
```

```python
import functools
import math

import numpy as np
import jax
import jax.numpy as jnp
from jax import lax
from jax.experimental import pallas as pl
from jax.experimental.pallas import tpu as pltpu

F32 = jnp.float32
BF16 = jnp.bfloat16

N_HEADS = 8
HEAD_DIM = 64
MIX_W = N_HEADS * HEAD_DIM
N_PAIRS = N_HEADS // 2
PAIR_W = 2 * HEAD_DIM
CHUNK = 64
CONV_W = 4
LORA_W = 256
ROPE_BASE = 10000.0
NORM_EPS = 1e-6
RWKV_GN_EPS = 64e-5
NEG_BIG = -1e30
HALO = 8

Z_COLS = 8192
(C_FQ, C_FK, C_FV, C_FO, C_RR, C_RK, C_RV, C_TQ, C_TK, C_TV, C_TG,
 C_MQ, C_MK, C_MV, C_MO) = range(15)
C_RL_256 = 7680 // LORA_W
C_GATE_128 = 7936 // 128
G_FOX_F, G_ML_I, G_ML_F = 0, 8, 16

VMEM_LIMIT = 56 * 1024 * 1024
CAST_BLOCK_ELEMS = 1 << 20


def _cparams(sem):
    return pltpu.CompilerParams(dimension_semantics=sem, vmem_limit_bytes=VMEM_LIMIT)


def _dot(a, b):
    return jnp.dot(a, b, preferred_element_type=F32)


def _dot_nt(a, b):
    return lax.dot_general(a, b, (((1,), (1,)), ((), ())), preferred_element_type=F32)


def _dot_tn(a, b):
    return lax.dot_general(a, b, (((0,), (0,)), ((), ())), preferred_element_type=F32)


def _split2(x):
    hi = x.astype(BF16)
    lo = (x - hi.astype(F32)).astype(BF16)
    return hi, lo


def _split3(x):
    hi = x.astype(BF16)
    r = x - hi.astype(F32)
    mid = r.astype(BF16)
    lo = (r - mid.astype(F32)).astype(BF16)
    return hi, mid, lo


def _dot_exact_lhs(m01, x):
    hi, mid, lo = _split3(x)
    return _dot(m01, hi) + _dot(m01, mid) + _dot(m01, lo)


def _dot_exact_rhs(x, m01):
    hi, mid, lo = _split3(x)
    return _dot(hi, m01) + _dot(mid, m01) + _dot(lo, m01)


def _seg_matrix(n):
    r = lax.broadcasted_iota(jnp.int32, (n, n), 0) // HEAD_DIM
    c = lax.broadcasted_iota(jnp.int32, (n, n), 1) // HEAD_DIM
    return jnp.where(r == c, 1.0, 0.0).astype(BF16)


def _seg_sum(x, seg):
    hi, lo = _split2(x)
    return _dot(hi, seg) + _dot(lo, seg)


def _sigmoid(x):
    return 1.0 / (1.0 + jnp.exp(-x))


def _silu(x):
    return x * _sigmoid(x)


def _rms_rows(x, gain):
    ms = jnp.mean(x * x, axis=-1, keepdims=True)
    return x * lax.rsqrt(ms + NORM_EPS) * gain


def _lane_is_head0(shape):
    return lax.broadcasted_iota(jnp.int32, shape, len(shape) - 1) < HEAD_DIM


def _stack_heads(x, m0):
    z = jnp.zeros_like(x)
    return jnp.concatenate([jnp.where(m0, x, z), jnp.where(m0, z, x)], axis=0)


def _unstack_rows(x_st, m0):
    return jnp.where(m0, x_st[:CHUNK], x_st[CHUNK:])


def _block_diag_mask():
    r = lax.broadcasted_iota(jnp.int32, (PAIR_W, PAIR_W), 0)
    c = lax.broadcasted_iota(jnp.int32, (PAIR_W, PAIR_W), 1)
    return (r // HEAD_DIM) == (c // HEAD_DIM), r % HEAD_DIM, c % HEAD_DIM


def _ffn_body(x_ref, g_ref, w1_ref, w3_ref, w2_ref, o_ref, h_ref):
    @pl.when(pl.program_id(1) == 0)
    def _():
        x = x_ref[...]
        h_ref[...] = _rms_rows(x, g_ref[...]).astype(BF16)
        o_ref[...] = x

    h = h_ref[...]
    half = w1_ref.shape[1] // 2
    parts = []
    for c in range(2):
        cols = slice(c * half, (c + 1) * half)
        a = _dot(h, w1_ref[:, cols])
        b = _dot(h, w3_ref[:, cols])
        g = (0.5 * _silu(a)) * b
        parts.append(_dot(g.astype(BF16), w2_ref[cols, :]))
    o_ref[...] += parts[0] + parts[1]


def _ffn(x, gain, w1, w3, w2, tm, tf):
    s, d = x.shape
    dff = w1.shape[1]
    return pl.pallas_call(
        _ffn_body,
        grid=(s // tm, dff // tf),
        in_specs=[
            pl.BlockSpec((tm, d), lambda i, j: (i, 0)),
            pl.BlockSpec((1, d), lambda i, j: (0, 0)),
            pl.BlockSpec((d, tf), lambda i, j: (0, j)),
            pl.BlockSpec((d, tf), lambda i, j: (0, j)),
            pl.BlockSpec((tf, d), lambda i, j: (j, 0)),
        ],
        out_specs=pl.BlockSpec((tm, d), lambda i, j: (i, 0)),
        out_shape=jax.ShapeDtypeStruct((s, d), F32),
        scratch_shapes=[pltpu.VMEM((tm, d), BF16)],
        compiler_params=_cparams(("parallel", "arbitrary")),
        name="ffn",
    )(x, gain, w1, w3, w2)


def _inproj_body(x_ref, g_ref, w_ref, o_ref, h_ref):
    @pl.when(pl.program_id(1) == 0)
    def _():
        h_ref[...] = _rms_rows(x_ref[...], g_ref[...]).astype(BF16)

    o_ref[...] = _dot(h_ref[...], w_ref[...])


def _inproj(x, gain, w, tm, tn):
    s, d = x.shape
    n = w.shape[1]
    return pl.pallas_call(
        _inproj_body,
        grid=(s // tm, n // tn),
        in_specs=[
            pl.BlockSpec((tm, d), lambda i, j: (i, 0)),
            pl.BlockSpec((1, d), lambda i, j: (0, 0)),
            pl.BlockSpec((d, tn), lambda i, j: (0, j)),
        ],
        out_specs=pl.BlockSpec((tm, tn), lambda i, j: (i, j)),
        out_shape=jax.ShapeDtypeStruct((s, n), F32),
        scratch_shapes=[pltpu.VMEM((tm, d), BF16)],
        compiler_params=_cparams(("parallel", "arbitrary")),
        name="inproj",
    )(x, gain, w)


def _merge_body(x_ref, g_ref, y0, y1, y2, y3, wg0, wg1, wg2, wg3, bg_ref, wb_ref, wo_ref,
                o_ref, h_ref):
    @pl.when(pl.program_id(1) == 0)
    def _():
        x = x_ref[...]
        h_ref[...] = _rms_rows(x, g_ref[...]).astype(BF16)
        o_ref[...] = x

    h = h_ref[...]
    merged = None
    for n, (y_ref, wg_ref) in enumerate(((y0, wg0), (y1, wg1), (y2, wg2), (y3, wg3))):
        gate = _sigmoid(_dot(h, wg_ref[...]) + bg_ref[n:n + 1, :])
        term = gate * _dot(y_ref[...], wb_ref[n])
        merged = term if merged is None else merged + term
    o_ref[...] += _dot(merged.astype(BF16), wo_ref[...])


def _merge(x, gain, ys, wg, bg, wb, wo, tm, te):
    s, d = x.shape
    nb = len(ys)
    ne = d // te
    y_specs = [pl.BlockSpec((tm, MIX_W), lambda i, j: (i, 0)) for _ in range(nb)]
    wg_specs = [pl.BlockSpec((d, te), functools.partial(lambda i, j, n: (0, n * ne + j), n=n))
                for n in range(nb)]
    return pl.pallas_call(
        _merge_body,
        grid=(s // tm, ne),
        in_specs=[
            pl.BlockSpec((tm, d), lambda i, j: (i, 0)),
            pl.BlockSpec((1, d), lambda i, j: (0, 0)),
            *y_specs, *wg_specs,
            pl.BlockSpec((nb, te), lambda i, j: (0, j)),
            pl.BlockSpec((nb, MIX_W, te), lambda i, j: (0, 0, j)),
            pl.BlockSpec((te, d), lambda i, j: (j, 0)),
        ],
        out_specs=pl.BlockSpec((tm, d), lambda i, j: (i, 0)),
        out_shape=jax.ShapeDtypeStruct((s, d), F32),
        scratch_shapes=[pltpu.VMEM((tm, d), BF16)],
        compiler_params=_cparams(("parallel", "arbitrary")),
        name="merge",
    )(x, gain, *ys, wg, wg, wg, wg, bg, wb, wo)


def _rope_body(pos_ref, inv_ref, sgn_ref, cos_ref, sin_ref):
    ang = pos_ref[...] * inv_ref[...]
    cos_ref[...] = jnp.cos(ang)
    sin_ref[...] = jnp.sin(ang) * sgn_ref[...]


def _rope_tables(pos_col, tm):
    s = pos_col.shape[0]
    half = HEAD_DIM // 2
    inv = ROPE_BASE ** (-jnp.arange(0, HEAD_DIM, 2, dtype=F32) / HEAD_DIM)
    inv = jnp.tile(inv, PAIR_W // half)[None, :]
    sgn = jnp.tile(jnp.concatenate([-jnp.ones((half,), F32), jnp.ones((half,), F32)]), 2)[None, :]
    return pl.pallas_call(
        _rope_body,
        grid=(s // tm,),
        in_specs=[pl.BlockSpec((tm, 1), lambda i: (i, 0)),
                  pl.BlockSpec((1, PAIR_W), lambda i: (0, 0)),
                  pl.BlockSpec((1, PAIR_W), lambda i: (0, 0))],
        out_specs=[pl.BlockSpec((tm, PAIR_W), lambda i: (i, 0))] * 2,
        out_shape=[jax.ShapeDtypeStruct((s, PAIR_W), F32)] * 2,
        compiler_params=_cparams(("parallel",)),
        name="rope_tables",
    )(pos_col, inv, sgn)


def _gates_body(z_ref, bias_ref, g_ref, mu_ref, u_ref, mprev_ref, mulast_ref, urow_ref,
                carry_ref, mcarry_ref):
    tm = z_ref.shape[0]

    @pl.when(pl.program_id(0) == 0)
    def _():
        carry_ref[...] = jnp.zeros_like(carry_ref)
        mcarry_ref[...] = jnp.zeros_like(mcarry_ref)

    v = z_ref[...] + bias_ref[...]
    ls = jnp.minimum(v, 0.0) - jnp.log1p(jnp.exp(-jnp.abs(v)))
    r = lax.broadcasted_iota(jnp.int32, (tm, tm), 0)
    c = lax.broadcasted_iota(jnp.int32, (tm, tm), 1)
    tri = jnp.where(c <= r, 1.0, 0.0).astype(BF16)
    tri_chunk = jnp.where((c <= r) & ((r // CHUNK) == (c // CHUNK)), 1.0, 0.0).astype(BF16)
    run = _dot_exact_lhs(tri, ls) + carry_ref[...]
    loc = _dot_exact_lhs(tri_chunk, ls)
    lane = lax.broadcasted_iota(jnp.int32, (1, 128), 1)
    out = jnp.where(lane < G_ML_I, run, jnp.where(lane < G_ML_F, v, loc))
    carry_ref[...] = run[tm - 1:tm, :]
    g_ref[...] = out

    ml_lane = (lane >= G_ML_F) & (lane < G_ML_F + N_HEADS)
    b = jnp.where(ml_lane, loc, 0.0)
    u = jnp.where(ml_lane, pltpu.roll(v, G_ML_F - G_ML_I, 1) - loc, 0.0)
    row_in_chunk = lax.broadcasted_iota(jnp.int32, (tm, 128), 0) % CHUNK
    run_max = u
    shift = 1
    while shift < CHUNK:
        run_max = jnp.where(row_in_chunk >= shift,
                            jnp.maximum(run_max, pltpu.roll(run_max, shift, 0)), run_max)
        shift *= 2
    m = mcarry_ref[...]
    m_prev_rows, mu_last_rows = [], []
    for ci in range(tm // CHUNK):
        last = ci * CHUNK + CHUNK - 1
        mu_last = jnp.maximum(m, run_max[last:last + 1, :])
        m_prev_rows.append(jnp.broadcast_to(m, (CHUNK, 128)))
        mu_last_rows.append(jnp.broadcast_to(mu_last, (CHUNK, 128)))
        m = b[last:last + 1, :] + mu_last
    mcarry_ref[...] = m
    m_prev = jnp.concatenate(m_prev_rows, axis=0)
    mu_ref[...] = jnp.maximum(m_prev, run_max)
    u_ref[...] = u
    mprev_ref[...] = m_prev
    mulast_ref[...] = jnp.concatenate(mu_last_rows, axis=0)
    u_t = u.T
    for ci in range(tm // CHUNK):
        cols = slice(ci * CHUNK, (ci + 1) * CHUNK)
        for p in range(N_PAIRS):
            h = G_ML_F + 2 * p
            urow_ref[ci, p:p + 1, :] = jnp.concatenate([u_t[h:h + 1, cols], u_t[h + 1:h + 2, cols]], axis=1)


def _gates(z, bias, tm):
    s = z.shape[0]
    col = pl.BlockSpec((tm, 128), lambda i: (i, 0))
    col_shape = jax.ShapeDtypeStruct((s, 128), F32)
    return pl.pallas_call(
        _gates_body,
        grid=(s // tm,),
        in_specs=[pl.BlockSpec((tm, 128), lambda i: (i, C_GATE_128)),
                  pl.BlockSpec((1, 128), lambda i: (0, 0))],
        out_specs=[col, col, col, col, col,
                   pl.BlockSpec((tm // CHUNK, N_PAIRS, PAIR_W), lambda i: (i, 0, 0))],
        out_shape=[col_shape] * 5 + [jax.ShapeDtypeStruct((s // CHUNK, N_PAIRS, PAIR_W), F32)],
        scratch_shapes=[pltpu.VMEM((1, 128), F32), pltpu.VMEM((1, 128), F32)],
        compiler_params=_cparams(("arbitrary",)),
        name="gates",
    )(z, bias)


FOX_SLOT = 128
FOX_BIAS_LANE = HEAD_DIM
LOG2E = 1.4426950408889634
FOX_MAX_STATIC_BOUND = 30.0


def _fox_prep_body(zq_ref, zk_ref, zv_ref, g_ref, shift_ref, qg_ref, kg_ref, spread_ref, place_ref,
                   q1_ref, k1_ref, q_ref, k_ref, v_ref):
    seg = _seg_matrix(MIX_W)
    q = zq_ref[...]
    k = zk_ref[...]
    q_ms = _seg_sum(q * q, seg) * (1.0 / HEAD_DIM)
    k_ms = _seg_sum(k * k, seg) * (1.0 / HEAD_DIM)
    qn = (q * lax.rsqrt(q_ms + NORM_EPS) * qg_ref[...] * (HEAD_DIM ** -0.5 * LOG2E)).astype(BF16)
    kn = (k * lax.rsqrt(k_ms + NORM_EPS) * kg_ref[...]).astype(BF16)
    spread = spread_ref[...]
    f2 = g_ref[...] * LOG2E
    f_q = sum(_dot(part, place_ref[n]) for n, part in enumerate(_split3(f2 - shift_ref[...])))
    f_k = sum(_dot(part, place_ref[3 + n]) for n, part in enumerate(_split3(f2)))
    q_ref[...] = (_dot(qn, spread) + f_q + q1_ref[...]).astype(BF16)
    k_ref[...] = (_dot(kn, spread) - f_k + k1_ref[...]).astype(BF16)
    v_ref[...] = zv_ref[...].astype(BF16)


def _fox_layout_consts():
    spread = np.zeros((MIX_W, N_HEADS * FOX_SLOT), np.float32)
    place = np.zeros((6, 128, N_HEADS * FOX_SLOT), np.float32)
    q_ones = np.zeros((1, N_HEADS * FOX_SLOT), np.float32)
    k_ones = np.zeros((1, N_HEADS * FOX_SLOT), np.float32)
    for h in range(N_HEADS):
        for dd in range(HEAD_DIM):
            spread[h * HEAD_DIM + dd, h * FOX_SLOT + dd] = 1.0
        for n in range(3):
            place[n, G_FOX_F + h, h * FOX_SLOT + FOX_BIAS_LANE + n] = 1.0
            place[3 + n, G_FOX_F + h, h * FOX_SLOT + FOX_BIAS_LANE + 3 + n] = 1.0
            k_ones[0, h * FOX_SLOT + FOX_BIAS_LANE + n] = 1.0
            q_ones[0, h * FOX_SLOT + FOX_BIAS_LANE + 3 + n] = 1.0
    return (jnp.asarray(spread, BF16), jnp.asarray(place, BF16), jnp.asarray(q_ones), jnp.asarray(k_ones))


def _fox_prep(z, g, shift, q_gain, k_gain, tm):
    s = z.shape[0]
    wide = N_HEADS * FOX_SLOT
    spread, place, q_ones, k_ones = _fox_layout_consts()
    zspec = lambda c: pl.BlockSpec((tm, MIX_W), lambda i: (i, c))
    return pl.pallas_call(
        _fox_prep_body,
        grid=(s // tm,),
        in_specs=[zspec(C_FQ), zspec(C_FK), zspec(C_FV),
                  pl.BlockSpec((tm, 128), lambda i: (i, 0)),
                  pl.BlockSpec((1, 128), lambda i: (0, 0)),
                  pl.BlockSpec((1, MIX_W), lambda i: (0, 0)),
                  pl.BlockSpec((1, MIX_W), lambda i: (0, 0)),
                  pl.BlockSpec((MIX_W, wide), lambda i: (0, 0)),
                  pl.BlockSpec((6, 128, wide), lambda i: (0, 0, 0)),
                  pl.BlockSpec((1, wide), lambda i: (0, 0)),
                  pl.BlockSpec((1, wide), lambda i: (0, 0))],
        out_specs=[pl.BlockSpec((tm, wide), lambda i: (i, 0)),
                   pl.BlockSpec((tm, wide), lambda i: (i, 0)),
                   pl.BlockSpec((tm, MIX_W), lambda i: (i, 0))],
        out_shape=[jax.ShapeDtypeStruct((s, wide), BF16),
                   jax.ShapeDtypeStruct((s, wide), BF16),
                   jax.ShapeDtypeStruct((s, MIX_W), BF16)],
        compiler_params=_cparams(("parallel",)),
        name="fox_prep",
    )(z, z, z, g, shift, q_gain, k_gain, spread, place, q_ones, k_ones)


def _fox_body(it_ref, jt_ref, q_ref, k_ref, v_ref, og_ref, o_ref, m_ref, l_ref, acc_ref):
    t = pl.program_id(1)
    i = it_ref[t]
    j = jt_ref[t]
    tq, tk = q_ref.shape[0], k_ref.shape[0]
    m0 = _lane_is_head0((1, PAIR_W))

    @pl.when(j == 0)
    def _():
        m_ref[...] = jnp.full_like(m_ref, NEG_BIG)
        l_ref[...] = jnp.zeros_like(l_ref)
        acc_ref[...] = jnp.zeros_like(acc_ref)

    def step(on_diagonal):
        v = v_ref[...]
        pvs, alphas = [], []
        for e in range(2):
            slot = slice(e * FOX_SLOT, (e + 1) * FOX_SLOT)
            s = _dot_nt(q_ref[:, slot], k_ref[:, slot])
            if on_diagonal:
                row = lax.broadcasted_iota(jnp.int32, (tq, tk), 0)
                col = lax.broadcasted_iota(jnp.int32, (tq, tk), 1)
                s = jnp.where(col <= row, s, NEG_BIG)
            m_prev = m_ref[e]
            m_new = jnp.maximum(m_prev, jnp.max(s, axis=1, keepdims=True))
            alpha = jnp.exp2(m_prev - m_new)
            pr = jnp.exp2(s - m_new)
            l_ref[e] = alpha * l_ref[e] + jnp.sum(pr, axis=1, keepdims=True)
            m_ref[e] = m_new
            pvs.append(_dot(pr.astype(BF16), v))
            alphas.append(alpha)
        acc_ref[...] = (jnp.where(m0, alphas[0], alphas[1]) * acc_ref[...]
                        + jnp.where(m0, pvs[0], pvs[1]))

    @pl.when(j < i)
    def _():
        step(False)

    @pl.when(j == i)
    def _():
        step(True)
        l_pair = jnp.where(m0, l_ref[0], l_ref[1])
        o_ref[...] = (acc_ref[...] / l_pair * _sigmoid(og_ref[...])).astype(o_ref.dtype)


def _fox_bounded_body(it_ref, jt_ref, q_ref, k_ref, v_ref, og_ref, o_ref, acc_ref):
    t = pl.program_id(1)
    i = it_ref[t]
    j = jt_ref[t]
    tq, tk = q_ref.shape[0], k_ref.shape[0]

    @pl.when(j == 0)
    def _():
        acc_ref[...] = jnp.zeros_like(acc_ref)

    n_pairs = v_ref.shape[1] // PAIR_W

    def step(on_diagonal):
        for p in range(n_pairs):
            v = v_ref[:, p * PAIR_W:(p + 1) * PAIR_W]
            v_ones = jnp.concatenate([v, jnp.ones_like(v)], axis=1)
            for e in range(2):
                h = 2 * p + e
                slot = slice(h * FOX_SLOT, (h + 1) * FOX_SLOT)
                pr = jnp.exp2(_dot_nt(q_ref[:, slot], k_ref[:, slot]))
                if on_diagonal:
                    row = lax.broadcasted_iota(jnp.int32, (tq, tk), 0)
                    col = lax.broadcasted_iota(jnp.int32, (tq, tk), 1)
                    pr = jnp.where(col <= row, pr, 0.0)
                acc_ref[h] += _dot(pr.astype(BF16), v_ones)

    @pl.when(j < i)
    def _():
        step(False)

    @pl.when(j == i)
    def _():
        step(True)
        m0 = _lane_is_head0((1, PAIR_W))
        for p in range(n_pairs):
            cols = slice(p * PAIR_W, (p + 1) * PAIR_W)
            num = jnp.where(m0, acc_ref[2 * p, :, :PAIR_W], acc_ref[2 * p + 1, :, :PAIR_W])
            den = jnp.where(m0, acc_ref[2 * p, :, PAIR_W:], acc_ref[2 * p + 1, :, PAIR_W:])
            o_ref[:, cols] = (num / den * _sigmoid(og_ref[:, cols])).astype(o_ref.dtype)


def _fox_attention(q, k, v, z, tq, bounded):
    s = v.shape[0]
    nq = s // tq
    ii, jj = np.tril_indices(nq)
    it = jnp.asarray(ii, jnp.int32)
    jt = jnp.asarray(jj, jnp.int32)
    if bounded:
        body = _fox_bounded_body
        pairs_per_step = N_PAIRS
        scratch = [pltpu.VMEM((N_HEADS, tq, 2 * PAIR_W), F32)]
    else:
        body = _fox_body
        pairs_per_step = 1
        scratch = [pltpu.VMEM((2, tq, 1), F32), pltpu.VMEM((2, tq, 1), F32),
                   pltpu.VMEM((tq, PAIR_W), F32)]
    wide = pairs_per_step * 2 * FOX_SLOT
    narrow = pairs_per_step * PAIR_W
    fo_blk = C_FO * (MIX_W // narrow)
    grid_spec = pltpu.PrefetchScalarGridSpec(
        num_scalar_prefetch=2,
        grid=(N_PAIRS // pairs_per_step, len(ii)),
        in_specs=[
            pl.BlockSpec((tq, wide), lambda p, t, it, jt: (it[t], p)),
            pl.BlockSpec((tq, wide), lambda p, t, it, jt: (jt[t], p)),
            pl.BlockSpec((tq, narrow), lambda p, t, it, jt: (jt[t], p)),
            pl.BlockSpec((tq, narrow), lambda p, t, it, jt: (it[t], fo_blk + p)),
        ],
        out_specs=pl.BlockSpec((tq, narrow), lambda p, t, it, jt: (it[t], p)),
        scratch_shapes=scratch,
    )
    return pl.pallas_call(
        body,
        grid_spec=grid_spec,
        out_shape=jax.ShapeDtypeStruct((s, MIX_W), BF16),
        compiler_params=_cparams(("parallel", "arbitrary")),
        name="fox_attention_bounded" if bounded else "fox_attention",
    )(it, jt, q, k, v, z)


def _retention_body(zq_ref, zk_ref, zv_ref, zg_ref, cos_ref, sin_ref, intra_ref, qd_ref, kd_ref,
                    cd_ref, gain_ref, o_ref, st_ref, q_scr, k_scr, y_scr):
    tm = zq_ref.shape[0]

    @pl.when(pl.program_id(0) == 0)
    def _():
        st_ref[...] = jnp.zeros_like(st_ref)

    lane = lax.broadcasted_iota(jnp.int32, (1, PAIR_W), 1)
    m0 = lane < HEAD_DIM
    first_half = (lane & (HEAD_DIM // 2)) == 0
    cos = cos_ref[...]
    sin = sin_ref[...]
    for p in range(N_PAIRS):
        cs = slice(p * PAIR_W, (p + 1) * PAIR_W)
        for src, dst, scale in ((zq_ref, q_scr, 1.0), (zk_ref, k_scr, HEAD_DIM ** -0.5)):
            x = src[:, cs]
            swapped = jnp.where(first_half, pltpu.roll(x, PAIR_W - HEAD_DIM // 2, 1),
                                pltpu.roll(x, HEAD_DIM // 2, 1))
            dst[:, cs] = (x * cos + swapped * sin) * scale
    bd, _, _ = _block_diag_mask()

    pairs = range(N_PAIRS)
    pair_cols = [slice(p * PAIR_W, (p + 1) * PAIR_W) for p in pairs]

    def chunk(c, carry):
        rows = pl.ds(pl.multiple_of(c * CHUNK, CHUNK), CHUNK)
        q = [q_scr[rows, cols] for cols in pair_cols]
        k = [k_scr[rows, cols] for cols in pair_cols]
        vb = [zv_ref[rows, cols].astype(BF16) for cols in pair_cols]
        st = [st_ref[p] for p in pairs]
        sc = [_dot_nt(_stack_heads(q[p], m0).astype(BF16), k[p].astype(BF16)) * intra_ref[p]
              for p in pairs]
        o_inter = [_dot((q[p] * qd_ref[p]).astype(BF16), st[p].astype(BF16)) for p in pairs]
        upd = [_dot_tn((k[p] * kd_ref[p]).astype(BF16), vb[p]) for p in pairs]
        o_intra = [_unstack_rows(_dot(sc[p].astype(BF16), vb[p]), m0) for p in pairs]
        for p in pairs:
            y_scr[rows, pair_cols[p]] = o_intra[p] + o_inter[p]
            st_ref[p] = st[p] * cd_ref[p] + jnp.where(bd, upd[p], 0.0)
        return carry

    lax.fori_loop(0, tm // CHUNK, chunk, 0)
    y = y_scr[...]
    ms = _seg_sum(y * y, _seg_matrix(MIX_W)) * (1.0 / HEAD_DIM)
    o_ref[...] = (y * lax.rsqrt(ms + NORM_EPS) * gain_ref[...] * _silu(zg_ref[...])).astype(o_ref.dtype)


def _retention_tables():
    hh = jnp.arange(N_HEADS, dtype=F32)
    log_gamma = jnp.log1p(-jnp.exp2(-5.0 - hh))
    idx = jnp.arange(CHUNK, dtype=F32)
    intra = jnp.exp(log_gamma[:, None, None] * jnp.abs(idx[:, None] - idx[None, :]))
    q_decay = jnp.exp(log_gamma[:, None] * (idx + 1.0))
    k_decay = jnp.exp(log_gamma[:, None] * (CHUNK - 1.0 - idx))
    chunk_decay = jnp.exp(log_gamma * CHUNK)
    intra_st = intra.reshape(N_PAIRS, 2 * CHUNK, CHUNK)
    to_pair = lambda t: jnp.repeat(t.reshape(N_PAIRS, 2, CHUNK).transpose(0, 2, 1), HEAD_DIM, axis=2)
    cd = jnp.repeat(chunk_decay.reshape(N_PAIRS, 1, 2), HEAD_DIM, axis=2)
    return intra_st, to_pair(q_decay), to_pair(k_decay), cd


def _retention(z, cos, sin, gain, tm):
    s = z.shape[0]
    intra_st, qd, kd, cd = _retention_tables()
    zspec = lambda c: pl.BlockSpec((tm, MIX_W), lambda i: (i, c))
    full = lambda a: pl.BlockSpec(a.shape, lambda i: (0,) * a.ndim)
    return pl.pallas_call(
        _retention_body,
        grid=(s // tm,),
        in_specs=[zspec(C_TQ), zspec(C_TK), zspec(C_TV), zspec(C_TG),
                  pl.BlockSpec((tm, PAIR_W), lambda i: (i, 0)),
                  pl.BlockSpec((tm, PAIR_W), lambda i: (i, 0)),
                  full(intra_st), full(qd), full(kd), full(cd),
                  pl.BlockSpec((1, MIX_W), lambda i: (0, 0))],
        out_specs=pl.BlockSpec((tm, MIX_W), lambda i: (i, 0)),
        out_shape=jax.ShapeDtypeStruct((s, MIX_W), BF16),
        scratch_shapes=[pltpu.VMEM((N_PAIRS, PAIR_W, PAIR_W), F32),
                        pltpu.VMEM((tm, MIX_W), F32), pltpu.VMEM((tm, MIX_W), F32),
                        pltpu.VMEM((tm, MIX_W), F32)],
        compiler_params=_cparams(("arbitrary",)),
        name="retention",
    )(z, z, z, z, cos, sin, intra_st, qd, kd, cd, gain)


def _load_with_halo(ext_ref, cur_ref, halo_ref, first):
    halo = halo_ref[...]
    ext_ref[0:HALO, :] = jnp.where(first, jnp.zeros_like(halo), halo)
    ext_ref[HALO:, :] = cur_ref[...]


def _mlstm_body(zq_ref, zk_ref, hq_ref, hk_ref, zv_ref, zo_ref, g_ref, mu_ref, u_ref, mprev_ref,
                mulast_ref, urow_ref, spread_ref, cw_ref, cb_ref, gain_ref, o_ref, c_ref, n_ref,
                ext_scr, q_scr, k_scr, y_scr, mu_scr, inter_scr, floor_scr, w_scr, dec_scr):
    tm = zq_ref.shape[0]
    first = pl.program_id(0) == 0

    @pl.when(first)
    def _():
        c_ref[...] = jnp.zeros_like(c_ref)
        n_ref[...] = jnp.zeros_like(n_ref)

    lane_g = lax.broadcasted_iota(jnp.int32, (1, 128), 1)
    ml_lane = (lane_g >= G_ML_F) & (lane_g < G_ML_F + N_HEADS)
    spread = spread_ref[...]
    to_heads = lambda t: _dot_exact_rhs(jnp.where(ml_lane, t, 0.0), spread)
    mu = mu_ref[...]
    m_prev = mprev_ref[...]
    mu_last = mulast_ref[...]
    mu_scr[...] = to_heads(mu)
    inter_scr[...] = to_heads(jnp.exp(m_prev - mu))
    floor_scr[...] = to_heads(jnp.exp(-(g_ref[...] + mu)))
    w_scr[...] = to_heads(jnp.exp(u_ref[...] - mu_last))
    dec_scr[...] = to_heads(jnp.exp(m_prev - mu_last))

    for part, (cur, halo, dst, scale) in enumerate(((zq_ref, hq_ref, q_scr, 1.0),
                                                    (zk_ref, hk_ref, k_scr, HEAD_DIM ** -0.5))):
        _load_with_halo(ext_scr, cur, halo, first)
        cols = slice(part * MIX_W, (part + 1) * MIX_W)
        acc = cb_ref[:, cols]
        for tap in range(CONV_W):
            off = HALO - (CONV_W - 1) + tap
            acc = acc + cw_ref[tap:tap + 1, cols] * ext_scr[off:off + tm, :]
        dst[...] = _silu(acc) * scale

    lane = lax.broadcasted_iota(jnp.int32, (1, PAIR_W), 1)
    m0 = lane < HEAD_DIM
    bd, _, _ = _block_diag_mask()
    bd_ones = jnp.where(bd, 1.0, 0.0).astype(BF16)
    causal = (lax.broadcasted_iota(jnp.int32, (CHUNK, PAIR_W), 1) % HEAD_DIM
              <= lax.broadcasted_iota(jnp.int32, (CHUNK, PAIR_W), 0))

    pairs = range(N_PAIRS)
    pair_cols = [slice(p * PAIR_W, (p + 1) * PAIR_W) for p in pairs]

    def chunk(c, carry):
        rows = pl.ds(pl.multiple_of(c * CHUNK, CHUNK), CHUNK)
        first8 = pl.ds(pl.multiple_of(c * CHUNK, CHUNK), 8)
        u_rows = urow_ref[c]
        q = [q_scr[rows, cols] for cols in pair_cols]
        k = [k_scr[rows, cols] for cols in pair_cols]
        v = [zv_ref[rows, cols] for cols in pair_cols]
        cst = [c_ref[p] for p in pairs]
        nst = [n_ref[p] for p in pairs]
        qb = [q[p].astype(BF16) for p in pairs]
        sc = [_dot_nt(qb[p], _stack_heads(k[p], m0).astype(BF16)) for p in pairs]
        q_c = [_dot(qb[p], cst[p].astype(BF16)) for p in pairs]
        q_n = [_seg_sum(q[p] * nst[p], bd_ones) for p in pairs]
        d_mat = [jnp.exp(u_rows[p:p + 1, :] - mu_scr[rows, pair_cols[p]]) for p in pairs]
        pr = [jnp.where(causal, sc[p] * d_mat[p], 0.0).astype(BF16) for p in pairs]
        v_ones = [jnp.concatenate([_stack_heads(v[p], m0).astype(BF16), bd_ones], axis=1) for p in pairs]
        nd = [_dot(pr[p], v_ones[p]) for p in pairs]
        kw = [k[p] * w_scr[rows, pair_cols[p]] for p in pairs]
        upd = [_dot_tn(kw[p].astype(BF16), v[p].astype(BF16)) for p in pairs]
        for p in pairs:
            inter = inter_scr[rows, pair_cols[p]]
            num = nd[p][:, :PAIR_W] + inter * q_c[p]
            den = nd[p][:, PAIR_W:] + inter * q_n[p]
            y_scr[rows, pair_cols[p]] = num / jnp.maximum(jnp.abs(den), floor_scr[rows, pair_cols[p]])
            dec = dec_scr[first8, pair_cols[p]][0:1]
            c_ref[p] = dec * cst[p] + jnp.where(bd, upd[p], 0.0)
            n_ref[p] = dec * nst[p] + jnp.sum(kw[p], axis=0, keepdims=True)
        return carry

    lax.fori_loop(0, tm // CHUNK, chunk, 0)
    y = y_scr[...] * _sigmoid(zo_ref[...])
    ms = _seg_sum(y * y, _seg_matrix(MIX_W)) * (1.0 / HEAD_DIM)
    o_ref[...] = (y * lax.rsqrt(ms + NORM_EPS) * gain_ref[...]).astype(o_ref.dtype)


def _halo_spec(tm, width, c):
    step = tm // HALO
    return pl.BlockSpec((HALO, width), lambda i: (jnp.maximum(i * step - 1, 0), c))


def _mlstm(z, gate_cols, urow, conv_w, conv_b, gain, tm):
    s = z.shape[0]
    spread = np.zeros((128, MIX_W), np.float32)
    for h in range(N_HEADS):
        spread[G_ML_F + h, h * HEAD_DIM:(h + 1) * HEAD_DIM] = 1.0
    zspec = lambda c: pl.BlockSpec((tm, MIX_W), lambda i: (i, c))
    col = pl.BlockSpec((tm, 128), lambda i: (i, 0))
    big = pltpu.VMEM((tm, MIX_W), F32)
    return pl.pallas_call(
        _mlstm_body,
        grid=(s // tm,),
        in_specs=[zspec(C_MQ), zspec(C_MK), _halo_spec(tm, MIX_W, C_MQ), _halo_spec(tm, MIX_W, C_MK),
                  zspec(C_MV), zspec(C_MO),
                  col, col, col, col, col,
                  pl.BlockSpec((tm // CHUNK, N_PAIRS, PAIR_W), lambda i: (i, 0, 0)),
                  pl.BlockSpec((128, MIX_W), lambda i: (0, 0)),
                  pl.BlockSpec((CONV_W, 2 * MIX_W), lambda i: (0, 0)),
                  pl.BlockSpec((1, 2 * MIX_W), lambda i: (0, 0)),
                  pl.BlockSpec((1, MIX_W), lambda i: (0, 0))],
        out_specs=pl.BlockSpec((tm, MIX_W), lambda i: (i, 0)),
        out_shape=jax.ShapeDtypeStruct((s, MIX_W), BF16),
        scratch_shapes=[pltpu.VMEM((N_PAIRS, PAIR_W, PAIR_W), F32),
                        pltpu.VMEM((N_PAIRS, 1, PAIR_W), F32),
                        pltpu.VMEM((tm + HALO, MIX_W), F32),
                        big, big, big, big, big, big, big, big],
        compiler_params=_cparams(("arbitrary",)),
        name="mlstm",
    )(z, z, z, z, z, z, *gate_cols, urow, jnp.asarray(spread, BF16), conv_w, conv_b, gain)


def _rwkv_body(zr_ref, zk_ref, zv_ref, zl_ref, hr_ref, hk_ref, hv_ref, hl_ref, mu_ref, mul_ref,
               wl_ref, w0_ref, a0_ref, kk_ref, ka_ref, rk_ref, gnw_ref, gnb_ref, o_ref,
               h_ref, ext_scr, extl_scr, rt_scr, kt_scr, kb_scr, bb_scr, kh_scr, bh_scr,
               gam_scr, v_scr, y_scr):
    tm = zr_ref.shape[0]
    first = pl.program_id(0) == 0

    @pl.when(first)
    def _():
        h_ref[...] = jnp.zeros_like(h_ref)

    def shifted(ext, cur_ref, halo_ref, mu):
        _load_with_halo(ext, cur_ref, halo_ref, first)
        cur = ext[HALO:HALO + tm, :]
        prev = ext[HALO - 1:HALO - 1 + tm, :]
        return cur + mu * (prev - cur)

    r = shifted(ext_scr, zr_ref, hr_ref, mu_ref[:, 0:MIX_W])
    k = shifted(ext_scr, zk_ref, hk_ref, mu_ref[:, MIX_W:2 * MIX_W])
    v = shifted(ext_scr, zv_ref, hv_ref, mu_ref[:, 2 * MIX_W:3 * MIX_W])
    lora_in = shifted(extl_scr, zl_ref, hl_ref, mul_ref[...])
    lane_l = lax.broadcasted_iota(jnp.int32, (1, LORA_W), 1)
    act = jnp.where(lane_l < 64, jnp.tanh(lora_in), jnp.where(lane_l < 128, lora_in, _sigmoid(lora_in)))
    lora = _dot(act.astype(BF16), wl_ref[...])
    logw = -_sigmoid(w0_ref[...] + lora[:, 0:MIX_W]) * math.exp(-0.5)
    a = _sigmoid(a0_ref[...] + lora[:, MIX_W:2 * MIX_W])
    gate = lora[:, 2 * MIX_W:3 * MIX_W]

    seg = _seg_matrix(MIX_W)
    kk = k * kk_ref[...]
    kk = kk * lax.rsqrt(jnp.maximum(_seg_sum(kk * kk, seg), 1e-12))
    k2 = k * (1.0 + (a - 1.0) * ka_ref[...])
    b = kk * a
    bonus = _seg_sum(r * k2 * rk_ref[...], seg) * v

    ri = lax.broadcasted_iota(jnp.int32, (tm, tm), 0)
    ci = lax.broadcasted_iota(jnp.int32, (tm, tm), 1)
    same = (ri // CHUNK) == (ci // CHUNK)
    lower = jnp.where(same & (ci <= ri), 1.0, 0.0).astype(BF16)
    upper = jnp.where(same & (ci > ri), 1.0, 0.0).astype(BF16)
    cl = _dot_exact_lhs(lower, logw)
    cs = _dot_exact_lhs(upper, logw)
    e_cl = jnp.exp(cl)
    e_ncl = jnp.exp(-cl)
    e_cs = jnp.exp(cs)
    rt_scr[...] = r * e_cl
    kt_scr[...] = kk * jnp.exp(cl - logw)
    kb_scr[...] = k2 * e_ncl
    bb_scr[...] = b * e_ncl
    kh_scr[...] = k2 * e_cs
    bh_scr[...] = b * e_cs
    gam_scr[...] = jnp.exp(cl + cs)
    v_scr[...] = v

    lane = lax.broadcasted_iota(jnp.int32, (1, PAIR_W), 1)
    m0 = lane < HEAD_DIM
    bd, rr, cc = _block_diag_mask()
    strict = bd & (cc < rr)
    incl = bd & (cc <= rr)
    eye = bd & (cc == rr)
    eye_f = jnp.where(eye, 1.0, 0.0)

    pairs = range(N_PAIRS)
    pair_cols = [slice(p * PAIR_W, (p + 1) * PAIR_W) for p in pairs]

    chunks_per_trip = 2
    units = [(cc, p) for cc in range(chunks_per_trip) for p in pairs]

    def trip(c, carry):
        base = pl.multiple_of(c * (chunks_per_trip * CHUNK), chunks_per_trip * CHUNK)
        rows = [pl.ds(base + cc * CHUNK, CHUNK) for cc in range(chunks_per_trip)]
        first8 = [pl.ds(base + cc * CHUNK, 8) for cc in range(chunks_per_trip)]
        stacked = lambda scr: [_stack_heads(scr[rows[cc], pair_cols[p]], m0) for cc, p in units]
        rt_st = stacked(rt_scr)
        kt_st = stacked(kt_scr)
        v_st = [t.astype(BF16) for t in stacked(v_scr)]
        kh_st = [t.astype(BF16) for t in stacked(kh_scr)]
        bh_st = [t.astype(BF16) for t in stacked(bh_scr)]
        twice = lambda scr: [jnp.concatenate([scr[rows[cc], pair_cols[p]].astype(BF16)] * 2, axis=0)
                             for cc, p in units]
        bb2 = twice(bb_scr)
        kb2 = twice(kb_scr)
        gam = [gam_scr[first8[cc], pair_cols[p]][0:1] for cc, p in units]
        us = range(len(units))
        lhs = [jnp.concatenate([kt_st[u], rt_st[u]], axis=0).astype(BF16) for u in us]
        g_b = [_dot_nt(lhs[u], bb2[u]) for u in us]
        g_k = [_dot_nt(lhs[u], kb2[u]) for u in us]
        x = [jnp.where(strict, -g_b[u][:PAIR_W], 0.0) for u in us]
        q_bd = [jnp.where(incl, g_b[u][PAIR_W:], 0.0).astype(BF16) for u in us]
        b_bd = [jnp.where(strict, g_k[u][:PAIR_W], 0.0).astype(BF16) for u in us]
        p_bd = [jnp.where(incl, g_k[u][PAIR_W:], 0.0).astype(BF16) for u in us]
        bv = [_dot(b_bd[u], v_st[u]).astype(BF16) for u in us]
        pv = [_dot(p_bd[u], v_st[u]) for u in us]
        khv = [_dot_tn(kh_st[u], v_st[u]) for u in us]
        t_inv = [eye_f + x[u] for u in us]
        for _ in range(5):
            xb = [x[u].astype(BF16) for u in us]
            x = [_dot(xb[u], xb[u]) for u in us]
            t_inv = [t_inv[u] + _dot(t_inv[u].astype(BF16), x[u].astype(BF16)) for u in us]
        tb = [t_inv[u].astype(BF16) for u in us]
        kt2 = [_dot(tb[u], kt_st[u].astype(BF16)).astype(BF16) for u in us]
        w1 = [_dot(tb[u], bv[u]).astype(BF16) for u in us]
        y1 = [(rt_st[u] - _dot(q_bd[u], kt2[u])).astype(BF16) for u in us]
        y0 = [pv[u] - _dot(q_bd[u], w1[u]) for u in us]
        m_mat = [(jnp.where(eye, gam[u], 0.0) - _dot_tn(bh_st[u], kt2[u])).astype(BF16) for u in us]
        n_mat = [khv[u] - _dot_tn(bh_st[u], w1[u]) for u in us]
        h = [h_ref[p] for p in pairs]
        for u, (cc, p) in enumerate(units):
            hb = h[p].astype(BF16)
            y_st = _dot(y1[u], hb) + y0[u]
            y_scr[rows[cc], pair_cols[p]] = y_st[:CHUNK] + y_st[CHUNK:]
            h[p] = _dot(m_mat[u], hb) + n_mat[u]
        for p in pairs:
            h_ref[p] = h[p]
        return carry

    lax.fori_loop(0, tm // (chunks_per_trip * CHUNK), trip, 0)

    y = y_scr[...]
    mean = _seg_sum(y, seg) * (1.0 / HEAD_DIM)
    yc = y - mean
    var = _seg_sum(yc * yc, seg) * (1.0 / HEAD_DIM)
    yn = yc * lax.rsqrt(var + RWKV_GN_EPS) * gnw_ref[...] + gnb_ref[...]
    o_ref[...] = ((yn + bonus) * gate).astype(o_ref.dtype)


def _rwkv(z, mu, mu_l, w_lora, w0, a0, k_k, k_a, r_k, gn_w, gn_b, tm):
    s = z.shape[0]
    zspec = lambda c: pl.BlockSpec((tm, MIX_W), lambda i: (i, c))
    row = lambda w: pl.BlockSpec((1, w), lambda i: (0, 0))
    big = pltpu.VMEM((tm, MIX_W), F32)
    return pl.pallas_call(
        _rwkv_body,
        grid=(s // tm,),
        in_specs=[zspec(C_RR), zspec(C_RK), zspec(C_RV),
                  pl.BlockSpec((tm, LORA_W), lambda i: (i, C_RL_256)),
                  _halo_spec(tm, MIX_W, C_RR), _halo_spec(tm, MIX_W, C_RK), _halo_spec(tm, MIX_W, C_RV),
                  _halo_spec(tm, LORA_W, C_RL_256),
                  row(3 * MIX_W), row(LORA_W),
                  pl.BlockSpec((LORA_W, 3 * MIX_W), lambda i: (0, 0)),
                  row(MIX_W), row(MIX_W), row(MIX_W), row(MIX_W), row(MIX_W), row(MIX_W), row(MIX_W)],
        out_specs=pl.BlockSpec((tm, MIX_W), lambda i: (i, 0)),
        out_shape=jax.ShapeDtypeStruct((s, MIX_W), BF16),
        scratch_shapes=[pltpu.VMEM((N_PAIRS, PAIR_W, PAIR_W), F32),
                        pltpu.VMEM((tm + HALO, MIX_W), F32), pltpu.VMEM((tm + HALO, LORA_W), F32),
                        big, big, big, big, big, big, big, big, big],
        compiler_params=_cparams(("arbitrary",)),
        name="rwkv7",
    )(z, z, z, z, z, z, z, z, mu, mu_l, w_lora, w0, a0, k_k, k_a, r_k, gn_w, gn_b)


def _w_in_column_map():
    fox, rwkv, ret, ml = 0, 2056, 3848, 5896
    src = np.full((Z_COLS,), -1, np.int64)

    def put(dst, start, n):
        src[dst:dst + n] = np.arange(start, start + n)

    for blk, start in ((C_FQ, fox), (C_FK, fox + 512), (C_FV, fox + 1024), (C_FO, fox + 1536),
                       (C_RR, rwkv), (C_RK, rwkv + 512), (C_RV, rwkv + 1024),
                       (C_TQ, ret), (C_TK, ret + 512), (C_TV, ret + 1024), (C_TG, ret + 1536),
                       (C_MQ, ml), (C_MK, ml + 512), (C_MV, ml + 1024), (C_MO, ml + 1536)):
        put(blk * MIX_W, start, MIX_W)
    put(C_RL_256 * LORA_W, rwkv + 1536, LORA_W)
    gate0 = C_GATE_128 * 128
    put(gate0 + G_FOX_F, fox + 2048, N_HEADS)
    put(gate0 + G_ML_I, ml + 2048, N_HEADS)
    put(gate0 + G_ML_F, ml + 2056, N_HEADS)
    return src


def _w_in_runs():
    src = _w_in_column_map()
    runs, start = [], 0
    for pos in range(1, Z_COLS + 1):
        run_continues = pos < Z_COLS and (
            (src[pos] < 0 and src[pos - 1] < 0) or (src[pos - 1] >= 0 and src[pos] == src[pos - 1] + 1))
        if not run_continues:
            runs.append((start, int(src[start]), pos - start))
            start = pos
    return runs


def _w_in_body(w_ref, o_ref):
    for dst, src, n in _w_in_runs():
        if src < 0:
            o_ref[:, dst:dst + n] = jnp.zeros((o_ref.shape[0], n), BF16)
        else:
            o_ref[:, dst:dst + n] = w_ref[:, src:src + n].astype(BF16)


def _permute_w_in(w_in, layer, tr):
    _, d, d_in = w_in.shape
    return pl.pallas_call(
        _w_in_body,
        grid=(d // tr,),
        in_specs=[pl.BlockSpec((None, tr, d_in), lambda i: (layer, i, 0))],
        out_specs=pl.BlockSpec((tr, Z_COLS), lambda i: (i, 0)),
        out_shape=jax.ShapeDtypeStruct((d, Z_COLS), BF16),
        compiler_params=_cparams(("parallel",)),
        name="w_in_layout",
    )(w_in)


def _merge_gate_body(w_ref, o_ref):
    d = w_ref.shape[2]
    for n in range(w_ref.shape[1]):
        o_ref[:, n * d:(n + 1) * d] = w_ref[:, n, :].astype(BF16)


def _merge_gate_bf16(w, layer, tr):
    _, r, nb, c = w.shape
    return pl.pallas_call(
        _merge_gate_body,
        grid=(r // tr,),
        in_specs=[pl.BlockSpec((None, tr, nb, c), lambda i: (layer, i, 0, 0))],
        out_specs=pl.BlockSpec((tr, nb * c), lambda i: (i, 0)),
        out_shape=jax.ShapeDtypeStruct((r, nb * c), BF16),
        compiler_params=_cparams(("parallel",)),
        name="merge_gate_bf16",
    )(w)


def _branch_bf16(w, layer):
    _, nb, r, c = w.shape
    return pl.pallas_call(
        _cast_body,
        grid=(nb,),
        in_specs=[pl.BlockSpec((None, None, r, c), lambda n: (layer, n, 0, 0))],
        out_specs=pl.BlockSpec((None, r, c), lambda n: (n, 0, 0)),
        out_shape=jax.ShapeDtypeStruct((nb, r, c), BF16),
        compiler_params=_cparams(("parallel",)),
        name="branch_bf16",
    )(w)


def _cast_body(w_ref, o_ref):
    o_ref[...] = w_ref[...].astype(BF16)


def _layer_bf16(w, layer):
    _, r, c = w.shape
    tr = r
    while tr * c > CAST_BLOCK_ELEMS and tr % 16 == 0:
        tr //= 2
    return pl.pallas_call(
        _cast_body,
        grid=(r // tr,),
        in_specs=[pl.BlockSpec((None, tr, c), lambda i: (layer, i, 0))],
        out_specs=pl.BlockSpec((tr, c), lambda i: (i, 0)),
        out_shape=jax.ShapeDtypeStruct((r, c), BF16),
        compiler_params=_cparams(("parallel",)),
        name="weight_bf16",
    )(w)


def kernel(x, positions, ffn1_norm, ffn1_w1, ffn1_w3, ffn1_w2, mix_norm, w_in, fox_f_bias, fox_q_gain, fox_k_gain, rwkv_shift_mu, rwkv_w0, rwkv_w2, rwkv_a0, rwkv_a2, rwkv_g2, rwkv_k_k, rwkv_k_a, rwkv_r_k, rwkv_gn_w, rwkv_gn_b, ret_gn_gain, mlstm_conv_w, mlstm_conv_b, mlstm_i_bias, mlstm_f_bias, mlstm_gn_gain, w_merge_gate, merge_gate_bias, w_branch, w_out, ffn2_norm, ffn2_w1, ffn2_w3, ffn2_w2):
    batch, seq, d = x.shape
    depth = w_in.shape[0]
    dff = ffn1_w1.shape[-1]
    pick = lambda full, want: want if full % want == 0 else full
    tm_dense = pick(seq, 512)
    tm_ffn = pick(seq, 1024)
    tm_inproj = pick(seq, 1024)
    tf = pick(dff, 512)
    te = pick(d, 512)
    tn = pick(Z_COLS, 1024)
    tm_row = pick(seq, 512)
    tm_seq = pick(seq, 256)
    tq = pick(seq, 512)
    row = lambda t: t.reshape(1, -1).astype(F32)

    outs = []
    for bi in range(batch):
        xb = x[bi]
        pos_col = positions[bi].astype(F32).reshape(seq, 1)
        cos, sin = _rope_tables(pos_col, tm_row)
        for l in range(depth):
            xb = _ffn(xb, row(ffn1_norm[l]), _layer_bf16(ffn1_w1, l), _layer_bf16(ffn1_w3, l),
                      _layer_bf16(ffn1_w2, l), tm_ffn, tf)
            z = _inproj(xb, row(mix_norm[l]), _permute_w_in(w_in, l, pick(d, 256)), tm_inproj, tn)

            gate_bias = jnp.zeros((128,), F32)
            gate_bias = gate_bias.at[G_FOX_F:G_FOX_F + N_HEADS].set(fox_f_bias[l])
            gate_bias = gate_bias.at[G_ML_I:G_ML_I + N_HEADS].set(mlstm_i_bias[l])
            gate_bias = gate_bias.at[G_ML_F:G_ML_F + N_HEADS].set(mlstm_f_bias[l])
            g, ml_mu, ml_u, ml_mprev, ml_mulast, ml_urow = _gates(z, gate_bias[None, :], tm_row)

            bound = (LOG2E * HEAD_DIM ** 0.5) * jnp.max(jnp.abs(fox_q_gain[l])) * jnp.max(jnp.abs(fox_k_gain[l]))
            bounded = bound <= FOX_MAX_STATIC_BOUND
            shift = jnp.where(bounded, bound, 0.0) * jnp.ones((1, 128), F32)
            fq, fk, fv = _fox_prep(z, g, shift, row(jnp.tile(fox_q_gain[l], N_HEADS)),
                                   row(jnp.tile(fox_k_gain[l], N_HEADS)), tm_row)
            y_fox = lax.cond(bounded,
                             lambda *a: _fox_attention(*a, tq, True),
                             lambda *a: _fox_attention(*a, tq, False), fq, fk, fv, z)

            mu = rwkv_shift_mu[l]
            w_lora = jnp.zeros((LORA_W, 3 * MIX_W), F32)
            w_lora = w_lora.at[0:64, 0:MIX_W].set(rwkv_w2[l])
            w_lora = w_lora.at[64:128, MIX_W:2 * MIX_W].set(rwkv_a2[l])
            w_lora = w_lora.at[128:256, 2 * MIX_W:3 * MIX_W].set(rwkv_g2[l])
            y_rwkv = _rwkv(z, row(mu[:3 * MIX_W]), row(mu[3 * MIX_W:]), w_lora.astype(BF16),
                           row(rwkv_w0[l]), row(rwkv_a0[l]), row(rwkv_k_k[l]), row(rwkv_k_a[l]),
                           row(rwkv_r_k[l]), row(rwkv_gn_w[l]), row(rwkv_gn_b[l]), tm_seq)

            y_ret = _retention(z, cos, sin, row(ret_gn_gain[l]), tm_seq)

            y_m = _mlstm(z, (g, ml_mu, ml_u, ml_mprev, ml_mulast), ml_urow,
                         mlstm_conv_w[l].astype(F32), row(mlstm_conv_b[l]), row(mlstm_gn_gain[l]), tm_seq)

            xb = _merge(xb, row(mix_norm[l]), (y_fox, y_rwkv, y_ret, y_m),
                        _merge_gate_bf16(w_merge_gate, l, pick(d, 64)), merge_gate_bias[l].astype(F32),
                        _branch_bf16(w_branch, l), _layer_bf16(w_out, l), tm_dense, te)
            xb = _ffn(xb, row(ffn2_norm[l]), _layer_bf16(ffn2_w1, l), _layer_bf16(ffn2_w3, l),
                      _layer_bf16(ffn2_w2, l), tm_ffn, tf)
        outs.append(xb)
    return jnp.stack(outs, axis=0)
```

```python
import functools
import math

import numpy as np
import jax
import jax.numpy as jnp
from jax import lax
from jax.experimental import pallas as pl
from jax.experimental.pallas import tpu as pltpu

F32 = jnp.float32
BF16 = jnp.bfloat16

N_HEADS = 8
HEAD_DIM = 64
MIX_W = N_HEADS * HEAD_DIM
N_PAIRS = N_HEADS // 2
PAIR_W = 2 * HEAD_DIM
CHUNK = 64
CONV_W = 4
LORA_W = 256
ROPE_BASE = 10000.0
NORM_EPS = 1e-6
RWKV_GN_EPS = 64e-5
NEG_BIG = -1e30
HALO = 8

Z_COLS = 8192
(C_FQ, C_FK, C_FV, C_FO, C_RR, C_RK, C_RV, C_TQ, C_TK, C_TV, C_TG,
 C_MQ, C_MK, C_MV, C_MO) = range(15)
C_RL_256 = 7680 // LORA_W
C_GATE_128 = 7936 // 128
G_FOX_F, G_ML_I, G_ML_F = 0, 8, 16

VMEM_LIMIT = 56 * 1024 * 1024
CAST_BLOCK_ELEMS = 1 << 20


def _cparams(sem):
    return pltpu.CompilerParams(dimension_semantics=sem, vmem_limit_bytes=VMEM_LIMIT)


def _dot(a, b):
    return jnp.dot(a, b, preferred_element_type=F32)


def _dot_nt(a, b):
    return lax.dot_general(a, b, (((1,), (1,)), ((), ())), preferred_element_type=F32)


def _dot_tn(a, b):
    return lax.dot_general(a, b, (((0,), (0,)), ((), ())), preferred_element_type=F32)


def _split2(x):
    hi = x.astype(BF16)
    lo = (x - hi.astype(F32)).astype(BF16)
    return hi, lo


def _split3(x):
    hi = x.astype(BF16)
    r = x - hi.astype(F32)
    mid = r.astype(BF16)
    lo = (r - mid.astype(F32)).astype(BF16)
    return hi, mid, lo


def _dot_exact_lhs(m01, x):
    hi, mid, lo = _split3(x)
    return _dot(m01, hi) + _dot(m01, mid) + _dot(m01, lo)


def _dot_exact_rhs(x, m01):
    hi, mid, lo = _split3(x)
    return _dot(hi, m01) + _dot(mid, m01) + _dot(lo, m01)


def _seg_matrix(n):
    r = lax.broadcasted_iota(jnp.int32, (n, n), 0) // HEAD_DIM
    c = lax.broadcasted_iota(jnp.int32, (n, n), 1) // HEAD_DIM
    return jnp.where(r == c, 1.0, 0.0).astype(BF16)


def _seg_sum(x, seg):
    hi, lo = _split2(x)
    return _dot(hi, seg) + _dot(lo, seg)


def _sigmoid(x):
    return 1.0 / (1.0 + jnp.exp(-x))


def _silu(x):
    return x * _sigmoid(x)


def _rms_rows(x, gain):
    ms = jnp.mean(x * x, axis=-1, keepdims=True)
    return x * lax.rsqrt(ms + NORM_EPS) * gain


def _lane_is_head0(shape):
    return lax.broadcasted_iota(jnp.int32, shape, len(shape) - 1) < HEAD_DIM


def _stack_heads(x, m0):
    z = jnp.zeros_like(x)
    return jnp.concatenate([jnp.where(m0, x, z), jnp.where(m0, z, x)], axis=0)


def _unstack_rows(x_st, m0):
    return jnp.where(m0, x_st[:CHUNK], x_st[CHUNK:])


def _block_diag_mask():
    r = lax.broadcasted_iota(jnp.int32, (PAIR_W, PAIR_W), 0)
    c = lax.broadcasted_iota(jnp.int32, (PAIR_W, PAIR_W), 1)
    return (r // HEAD_DIM) == (c // HEAD_DIM), r % HEAD_DIM, c % HEAD_DIM


def _ffn_body(x_ref, g_ref, w1_ref, w3_ref, w2_ref, o_ref, h_ref):
    @pl.when(pl.program_id(1) == 0)
    def _():
        x = x_ref[...]
        h_ref[...] = _rms_rows(x, g_ref[...]).astype(BF16)
        o_ref[...] = x

    h = h_ref[...]
    half = w1_ref.shape[1] // 2
    parts = []
    for c in range(2):
        cols = slice(c * half, (c + 1) * half)
        a = _dot(h, w1_ref[:, cols])
        b = _dot(h, w3_ref[:, cols])
        g = (0.5 * _silu(a)) * b
        parts.append(_dot(g.astype(BF16), w2_ref[cols, :]))
    o_ref[...] += parts[0] + parts[1]


def _ffn(x, gain, w1, w3, w2, tm, tf):
    s, d = x.shape
    dff = w1.shape[1]
    return pl.pallas_call(
        _ffn_body,
        grid=(s // tm, dff // tf),
        in_specs=[
            pl.BlockSpec((tm, d), lambda i, j: (i, 0)),
            pl.BlockSpec((1, d), lambda i, j: (0, 0)),
            pl.BlockSpec((d, tf), lambda i, j: (0, j)),
            pl.BlockSpec((d, tf), lambda i, j: (0, j)),
            pl.BlockSpec((tf, d), lambda i, j: (j, 0)),
        ],
        out_specs=pl.BlockSpec((tm, d), lambda i, j: (i, 0)),
        out_shape=jax.ShapeDtypeStruct((s, d), F32),
        scratch_shapes=[pltpu.VMEM((tm, d), BF16)],
        compiler_params=_cparams(("parallel", "arbitrary")),
        name="ffn",
    )(x, gain, w1, w3, w2)


def _inproj_body(x_ref, g_ref, w_ref, o_ref, h_ref):
    @pl.when(pl.program_id(1) == 0)
    def _():
        h_ref[...] = _rms_rows(x_ref[...], g_ref[...]).astype(BF16)

    o_ref[...] = _dot(h_ref[...], w_ref[...])


def _inproj(x, gain, w, tm, tn):
    s, d = x.shape
    n = w.shape[1]
    return pl.pallas_call(
        _inproj_body,
        grid=(s // tm, n // tn),
        in_specs=[
            pl.BlockSpec((tm, d), lambda i, j: (i, 0)),
            pl.BlockSpec((1, d), lambda i, j: (0, 0)),
            pl.BlockSpec((d, tn), lambda i, j: (0, j)),
        ],
        out_specs=pl.BlockSpec((tm, tn), lambda i, j: (i, j)),
        out_shape=jax.ShapeDtypeStruct((s, n), F32),
        scratch_shapes=[pltpu.VMEM((tm, d), BF16)],
        compiler_params=_cparams(("parallel", "arbitrary")),
        name="inproj",
    )(x, gain, w)


def _merge_body(x_ref, g_ref, y0, y1, y2, y3, wg0, wg1, wg2, wg3, bg_ref, wb_ref, wo_ref,
                o_ref, h_ref):
    @pl.when(pl.program_id(1) == 0)
    def _():
        x = x_ref[...]
        h_ref[...] = _rms_rows(x, g_ref[...]).astype(BF16)
        o_ref[...] = x

    h = h_ref[...]
    merged = None
    for n, (y_ref, wg_ref) in enumerate(((y0, wg0), (y1, wg1), (y2, wg2), (y3, wg3))):
        gate = _sigmoid(_dot(h, wg_ref[...]) + bg_ref[n:n + 1, :])
        term = gate * _dot(y_ref[...], wb_ref[n])
        merged = term if merged is None else merged + term
    o_ref[...] += _dot(merged.astype(BF16), wo_ref[...])


def _merge(x, gain, ys, wg, bg, wb, wo, tm, te):
    s, d = x.shape
    nb = len(ys)
    ne = d // te
    y_specs = [pl.BlockSpec((tm, MIX_W), lambda i, j: (i, 0)) for _ in range(nb)]
    wg_specs = [pl.BlockSpec((d, te), functools.partial(lambda i, j, n: (0, n * ne + j), n=n))
                for n in range(nb)]
    return pl.pallas_call(
        _merge_body,
        grid=(s // tm, ne),
        in_specs=[
            pl.BlockSpec((tm, d), lambda i, j: (i, 0)),
            pl.BlockSpec((1, d), lambda i, j: (0, 0)),
            *y_specs, *wg_specs,
            pl.BlockSpec((nb, te), lambda i, j: (0, j)),
            pl.BlockSpec((nb, MIX_W, te), lambda i, j: (0, 0, j)),
            pl.BlockSpec((te, d), lambda i, j: (j, 0)),
        ],
        out_specs=pl.BlockSpec((tm, d), lambda i, j: (i, 0)),
        out_shape=jax.ShapeDtypeStruct((s, d), F32),
        scratch_shapes=[pltpu.VMEM((tm, d), BF16)],
        compiler_params=_cparams(("parallel", "arbitrary")),
        name="merge",
    )(x, gain, *ys, wg, wg, wg, wg, bg, wb, wo)


def _rope_body(pos_ref, inv_ref, sgn_ref, cos_ref, sin_ref):
    ang = pos_ref[...] * inv_ref[...]
    cos_ref[...] = jnp.cos(ang)
    sin_ref[...] = jnp.sin(ang) * sgn_ref[...]


def _rope_tables(pos_col, tm):
    s = pos_col.shape[0]
    half = HEAD_DIM // 2
    inv = ROPE_BASE ** (-jnp.arange(0, HEAD_DIM, 2, dtype=F32) / HEAD_DIM)
    inv = jnp.tile(inv, PAIR_W // half)[None, :]
    sgn = jnp.tile(jnp.concatenate([-jnp.ones((half,), F32), jnp.ones((half,), F32)]), 2)[None, :]
    return pl.pallas_call(
        _rope_body,
        grid=(s // tm,),
        in_specs=[pl.BlockSpec((tm, 1), lambda i: (i, 0)),
                  pl.BlockSpec((1, PAIR_W), lambda i: (0, 0)),
                  pl.BlockSpec((1, PAIR_W), lambda i: (0, 0))],
        out_specs=[pl.BlockSpec((tm, PAIR_W), lambda i: (i, 0))] * 2,
        out_shape=[jax.ShapeDtypeStruct((s, PAIR_W), F32)] * 2,
        compiler_params=_cparams(("parallel",)),
        name="rope_tables",
    )(pos_col, inv, sgn)


def _gates_body(z_ref, bias_ref, g_ref, mu_ref, u_ref, mprev_ref, mulast_ref, urow_ref,
                carry_ref, mcarry_ref):
    tm = z_ref.shape[0]

    @pl.when(pl.program_id(0) == 0)
    def _():
        carry_ref[...] = jnp.zeros_like(carry_ref)
        mcarry_ref[...] = jnp.zeros_like(mcarry_ref)

    v = z_ref[...] + bias_ref[...]
    ls = jnp.minimum(v, 0.0) - jnp.log1p(jnp.exp(-jnp.abs(v)))
    r = lax.broadcasted_iota(jnp.int32, (tm, tm), 0)
    c = lax.broadcasted_iota(jnp.int32, (tm, tm), 1)
    tri = jnp.where(c <= r, 1.0, 0.0).astype(BF16)
    tri_chunk = jnp.where((c <= r) & ((r // CHUNK) == (c // CHUNK)), 1.0, 0.0).astype(BF16)
    run = _dot_exact_lhs(tri, ls) + carry_ref[...]
    loc = _dot_exact_lhs(tri_chunk, ls)
    lane = lax.broadcasted_iota(jnp.int32, (1, 128), 1)
    out = jnp.where(lane < G_ML_I, run, jnp.where(lane < G_ML_F, v, loc))
    carry_ref[...] = run[tm - 1:tm, :]
    g_ref[...] = out

    ml_lane = (lane >= G_ML_F) & (lane < G_ML_F + N_HEADS)
    b = jnp.where(ml_lane, loc, 0.0)
    u = jnp.where(ml_lane, pltpu.roll(v, G_ML_F - G_ML_I, 1) - loc, 0.0)
    row_in_chunk = lax.broadcasted_iota(jnp.int32, (tm, 128), 0) % CHUNK
    run_max = u
    shift = 1
    while shift < CHUNK:
        run_max = jnp.where(row_in_chunk >= shift,
                            jnp.maximum(run_max, pltpu.roll(run_max, shift, 0)), run_max)
        shift *= 2
    m = mcarry_ref[...]
    m_prev_rows, mu_last_rows = [], []
    for ci in range(tm // CHUNK):
        last = ci * CHUNK + CHUNK - 1
        mu_last = jnp.maximum(m, run_max[last:last + 1, :])
        m_prev_rows.append(jnp.broadcast_to(m, (CHUNK, 128)))
        mu_last_rows.append(jnp.broadcast_to(mu_last, (CHUNK, 128)))
        m = b[last:last + 1, :] + mu_last
    mcarry_ref[...] = m
    m_prev = jnp.concatenate(m_prev_rows, axis=0)
    mu_ref[...] = jnp.maximum(m_prev, run_max)
    u_ref[...] = u
    mprev_ref[...] = m_prev
    mulast_ref[...] = jnp.concatenate(mu_last_rows, axis=0)
    u_t = u.T
    for ci in range(tm // CHUNK):
        cols = slice(ci * CHUNK, (ci + 1) * CHUNK)
        for p in range(N_PAIRS):
            h = G_ML_F + 2 * p
            urow_ref[ci, p:p + 1, :] = jnp.concatenate([u_t[h:h + 1, cols], u_t[h + 1:h + 2, cols]], axis=1)


def _gates(z, bias, tm):
    s = z.shape[0]
    col = pl.BlockSpec((tm, 128), lambda i: (i, 0))
    col_shape = jax.ShapeDtypeStruct((s, 128), F32)
    return pl.pallas_call(
        _gates_body,
        grid=(s // tm,),
        in_specs=[pl.BlockSpec((tm, 128), lambda i: (i, C_GATE_128)),
                  pl.BlockSpec((1, 128), lambda i: (0, 0))],
        out_specs=[col, col, col, col, col,
                   pl.BlockSpec((tm // CHUNK, N_PAIRS, PAIR_W), lambda i: (i, 0, 0))],
        out_shape=[col_shape] * 5 + [jax.ShapeDtypeStruct((s // CHUNK, N_PAIRS, PAIR_W), F32)],
        scratch_shapes=[pltpu.VMEM((1, 128), F32), pltpu.VMEM((1, 128), F32)],
        compiler_params=_cparams(("arbitrary",)),
        name="gates",
    )(z, bias)


FOX_SLOT = 128
FOX_BIAS_LANE = HEAD_DIM
LOG2E = 1.4426950408889634
FOX_MAX_STATIC_BOUND = 30.0


def _fox_prep_body(zq_ref, zk_ref, zv_ref, g_ref, shift_ref, qg_ref, kg_ref, spread_ref, place_ref,
                   q1_ref, k1_ref, q_ref, k_ref, v_ref):
    seg = _seg_matrix(MIX_W)
    q = zq_ref[...]
    k = zk_ref[...]
    q_ms = _seg_sum(q * q, seg) * (1.0 / HEAD_DIM)
    k_ms = _seg_sum(k * k, seg) * (1.0 / HEAD_DIM)
    qn = (q * lax.rsqrt(q_ms + NORM_EPS) * qg_ref[...] * (HEAD_DIM ** -0.5 * LOG2E)).astype(BF16)
    kn = (k * lax.rsqrt(k_ms + NORM_EPS) * kg_ref[...]).astype(BF16)
    spread = spread_ref[...]
    f2 = g_ref[...] * LOG2E
    f_q = sum(_dot(part, place_ref[n]) for n, part in enumerate(_split3(f2 - shift_ref[...])))
    f_k = sum(_dot(part, place_ref[3 + n]) for n, part in enumerate(_split3(f2)))
    q_ref[...] = (_dot(qn, spread) + f_q + q1_ref[...]).astype(BF16)
    k_ref[...] = (_dot(kn, spread) - f_k + k1_ref[...]).astype(BF16)
    v_ref[...] = zv_ref[...].astype(BF16)


def _fox_layout_consts():
    spread = np.zeros((MIX_W, N_HEADS * FOX_SLOT), np.float32)
    place = np.zeros((6, 128, N_HEADS * FOX_SLOT), np.float32)
    q_ones = np.zeros((1, N_HEADS * FOX_SLOT), np.float32)
    k_ones = np.zeros((1, N_HEADS * FOX_SLOT), np.float32)
    for h in range(N_HEADS):
        for dd in range(HEAD_DIM):
            spread[h * HEAD_DIM + dd, h * FOX_SLOT + dd] = 1.0
        for n in range(3):
            place[n, G_FOX_F + h, h * FOX_SLOT + FOX_BIAS_LANE + n] = 1.0
            place[3 + n, G_FOX_F + h, h * FOX_SLOT + FOX_BIAS_LANE + 3 + n] = 1.0
            k_ones[0, h * FOX_SLOT + FOX_BIAS_LANE + n] = 1.0
            q_ones[0, h * FOX_SLOT + FOX_BIAS_LANE + 3 + n] = 1.0
    return (jnp.asarray(spread, BF16), jnp.asarray(place, BF16), jnp.asarray(q_ones), jnp.asarray(k_ones))


def _fox_prep(z, g, shift, q_gain, k_gain, tm):
    s = z.shape[0]
    wide = N_HEADS * FOX_SLOT
    spread, place, q_ones, k_ones = _fox_layout_consts()
    zspec = lambda c: pl.BlockSpec((tm, MIX_W), lambda i: (i, c))
    return pl.pallas_call(
        _fox_prep_body,
        grid=(s // tm,),
        in_specs=[zspec(C_FQ), zspec(C_FK), zspec(C_FV),
                  pl.BlockSpec((tm, 128), lambda i: (i, 0)),
                  pl.BlockSpec((1, 128), lambda i: (0, 0)),
                  pl.BlockSpec((1, MIX_W), lambda i: (0, 0)),
                  pl.BlockSpec((1, MIX_W), lambda i: (0, 0)),
                  pl.BlockSpec((MIX_W, wide), lambda i: (0, 0)),
                  pl.BlockSpec((6, 128, wide), lambda i: (0, 0, 0)),
                  pl.BlockSpec((1, wide), lambda i: (0, 0)),
                  pl.BlockSpec((1, wide), lambda i: (0, 0))],
        out_specs=[pl.BlockSpec((tm, wide), lambda i: (i, 0)),
                   pl.BlockSpec((tm, wide), lambda i: (i, 0)),
                   pl.BlockSpec((tm, MIX_W), lambda i: (i, 0))],
        out_shape=[jax.ShapeDtypeStruct((s, wide), BF16),
                   jax.ShapeDtypeStruct((s, wide), BF16),
                   jax.ShapeDtypeStruct((s, MIX_W), BF16)],
        compiler_params=_cparams(("parallel",)),
        name="fox_prep",
    )(z, z, z, g, shift, q_gain, k_gain, spread, place, q_ones, k_ones)


def _fox_body(it_ref, jt_ref, q_ref, k_ref, v_ref, og_ref, o_ref, m_ref, l_ref, acc_ref):
    t = pl.program_id(1)
    i = it_ref[t]
    j = jt_ref[t]
    tq, tk = q_ref.shape[0], k_ref.shape[0]
    m0 = _lane_is_head0((1, PAIR_W))

    @pl.when(j == 0)
    def _():
        m_ref[...] = jnp.full_like(m_ref, NEG_BIG)
        l_ref[...] = jnp.zeros_like(l_ref)
        acc_ref[...] = jnp.zeros_like(acc_ref)

    def step(on_diagonal):
        v = v_ref[...]
        pvs, alphas = [], []
        for e in range(2):
            slot = slice(e * FOX_SLOT, (e + 1) * FOX_SLOT)
            s = _dot_nt(q_ref[:, slot], k_ref[:, slot])
            if on_diagonal:
                row = lax.broadcasted_iota(jnp.int32, (tq, tk), 0)
                col = lax.broadcasted_iota(jnp.int32, (tq, tk), 1)
                s = jnp.where(col <= row, s, NEG_BIG)
            m_prev = m_ref[e]
            m_new = jnp.maximum(m_prev, jnp.max(s, axis=1, keepdims=True))
            alpha = jnp.exp2(m_prev - m_new)
            pr = jnp.exp2(s - m_new)
            l_ref[e] = alpha * l_ref[e] + jnp.sum(pr, axis=1, keepdims=True)
            m_ref[e] = m_new
            pvs.append(_dot(pr.astype(BF16), v))
            alphas.append(alpha)
        acc_ref[...] = (jnp.where(m0, alphas[0], alphas[1]) * acc_ref[...]
                        + jnp.where(m0, pvs[0], pvs[1]))

    @pl.when(j < i)
    def _():
        step(False)

    @pl.when(j == i)
    def _():
        step(True)
        l_pair = jnp.where(m0, l_ref[0], l_ref[1])
        o_ref[...] = (acc_ref[...] / l_pair * _sigmoid(og_ref[...])).astype(o_ref.dtype)


def _fox_bounded_body(it_ref, jt_ref, q_ref, k_ref, v_ref, og_ref, o_ref, acc_ref):
    t = pl.program_id(1)
    i = it_ref[t]
    j = jt_ref[t]
    tq, tk = q_ref.shape[0], k_ref.shape[0]

    @pl.when(j == 0)
    def _():
        acc_ref[...] = jnp.zeros_like(acc_ref)

    n_pairs = v_ref.shape[1] // PAIR_W

    def step(on_diagonal):
        for p in range(n_pairs):
            v = v_ref[:, p * PAIR_W:(p + 1) * PAIR_W]
            v_ones = jnp.concatenate([v, jnp.ones_like(v)], axis=1)
            for e in range(2):
                h = 2 * p + e
                slot = slice(h * FOX_SLOT, (h + 1) * FOX_SLOT)
                pr = jnp.exp2(_dot_nt(q_ref[:, slot], k_ref[:, slot]))
                if on_diagonal:
                    row = lax.broadcasted_iota(jnp.int32, (tq, tk), 0)
                    col = lax.broadcasted_iota(jnp.int32, (tq, tk), 1)
                    pr = jnp.where(col <= row, pr, 0.0)
                acc_ref[h] += _dot(pr.astype(BF16), v_ones)

    @pl.when(j < i)
    def _():
        step(False)

    @pl.when(j == i)
    def _():
        step(True)
        m0 = _lane_is_head0((1, PAIR_W))
        for p in range(n_pairs):
            cols = slice(p * PAIR_W, (p + 1) * PAIR_W)
            num = jnp.where(m0, acc_ref[2 * p, :, :PAIR_W], acc_ref[2 * p + 1, :, :PAIR_W])
            den = jnp.where(m0, acc_ref[2 * p, :, PAIR_W:], acc_ref[2 * p + 1, :, PAIR_W:])
            o_ref[:, cols] = (num / den * _sigmoid(og_ref[:, cols])).astype(o_ref.dtype)


def _fox_attention(q, k, v, z, tq, bounded):
    s = v.shape[0]
    nq = s // tq
    ii, jj = np.tril_indices(nq)
    it = jnp.asarray(ii, jnp.int32)
    jt = jnp.asarray(jj, jnp.int32)
    if bounded:
        body = _fox_bounded_body
        pairs_per_step = N_PAIRS
        scratch = [pltpu.VMEM((N_HEADS, tq, 2 * PAIR_W), F32)]
    else:
        body = _fox_body
        pairs_per_step = 1
        scratch = [pltpu.VMEM((2, tq, 1), F32), pltpu.VMEM((2, tq, 1), F32),
                   pltpu.VMEM((tq, PAIR_W), F32)]
    wide = pairs_per_step * 2 * FOX_SLOT
    narrow = pairs_per_step * PAIR_W
    fo_blk = C_FO * (MIX_W // narrow)
    grid_spec = pltpu.PrefetchScalarGridSpec(
        num_scalar_prefetch=2,
        grid=(N_PAIRS // pairs_per_step, len(ii)),
        in_specs=[
            pl.BlockSpec((tq, wide), lambda p, t, it, jt: (it[t], p)),
            pl.BlockSpec((tq, wide), lambda p, t, it, jt: (jt[t], p)),
            pl.BlockSpec((tq, narrow), lambda p, t, it, jt: (jt[t], p)),
            pl.BlockSpec((tq, narrow), lambda p, t, it, jt: (it[t], fo_blk + p)),
        ],
        out_specs=pl.BlockSpec((tq, narrow), lambda p, t, it, jt: (it[t], p)),
        scratch_shapes=scratch,
    )
    return pl.pallas_call(
        body,
        grid_spec=grid_spec,
        out_shape=jax.ShapeDtypeStruct((s, MIX_W), BF16),
        compiler_params=_cparams(("parallel", "arbitrary")),
        name="fox_attention_bounded" if bounded else "fox_attention",
    )(it, jt, q, k, v, z)


def _retention_body(zq_ref, zk_ref, zv_ref, zg_ref, cos_ref, sin_ref, intra_ref, qd_ref, kd_ref,
                    cd_ref, gain_ref, o_ref, st_ref, q_scr, k_scr, y_scr):
    tm = zq_ref.shape[0]

    @pl.when(pl.program_id(0) == 0)
    def _():
        st_ref[...] = jnp.zeros_like(st_ref)

    lane = lax.broadcasted_iota(jnp.int32, (1, PAIR_W), 1)
    m0 = lane < HEAD_DIM
    first_half = (lane & (HEAD_DIM // 2)) == 0
    cos = cos_ref[...]
    sin = sin_ref[...]
    for p in range(N_PAIRS):
        cs = slice(p * PAIR_W, (p + 1) * PAIR_W)
        for src, dst, scale in ((zq_ref, q_scr, 1.0), (zk_ref, k_scr, HEAD_DIM ** -0.5)):
            x = src[:, cs]
            swapped = jnp.where(first_half, pltpu.roll(x, PAIR_W - HEAD_DIM // 2, 1),
                                pltpu.roll(x, HEAD_DIM // 2, 1))
            dst[:, cs] = (x * cos + swapped * sin) * scale
    bd, _, _ = _block_diag_mask()

    pairs = range(N_PAIRS)
    pair_cols = [slice(p * PAIR_W, (p + 1) * PAIR_W) for p in pairs]

    def chunk(c, carry):
        rows = pl.ds(pl.multiple_of(c * CHUNK, CHUNK), CHUNK)
        q = [q_scr[rows, cols] for cols in pair_cols]
        k = [k_scr[rows, cols] for cols in pair_cols]
        vb = [zv_ref[rows, cols].astype(BF16) for cols in pair_cols]
        st = [st_ref[p] for p in pairs]
        sc = [_dot_nt(_stack_heads(q[p], m0).astype(BF16), k[p].astype(BF16)) * intra_ref[p]
              for p in pairs]
        o_inter = [_dot((q[p] * qd_ref[p]).astype(BF16), st[p].astype(BF16)) for p in pairs]
        upd = [_dot_tn((k[p] * kd_ref[p]).astype(BF16), vb[p]) for p in pairs]
        o_intra = [_unstack_rows(_dot(sc[p].astype(BF16), vb[p]), m0) for p in pairs]
        for p in pairs:
            y_scr[rows, pair_cols[p]] = o_intra[p] + o_inter[p]
            st_ref[p] = st[p] * cd_ref[p] + jnp.where(bd, upd[p], 0.0)
        return carry

    lax.fori_loop(0, tm // CHUNK, chunk, 0)
    y = y_scr[...]
    ms = _seg_sum(y * y, _seg_matrix(MIX_W)) * (1.0 / HEAD_DIM)
    o_ref[...] = (y * lax.rsqrt(ms + NORM_EPS) * gain_ref[...] * _silu(zg_ref[...])).astype(o_ref.dtype)


def _retention_tables():
    hh = jnp.arange(N_HEADS, dtype=F32)
    log_gamma = jnp.log1p(-jnp.exp2(-5.0 - hh))
    idx = jnp.arange(CHUNK, dtype=F32)
    intra = jnp.exp(log_gamma[:, None, None] * jnp.abs(idx[:, None] - idx[None, :]))
    q_decay = jnp.exp(log_gamma[:, None] * (idx + 1.0))
    k_decay = jnp.exp(log_gamma[:, None] * (CHUNK - 1.0 - idx))
    chunk_decay = jnp.exp(log_gamma * CHUNK)
    intra_st = intra.reshape(N_PAIRS, 2 * CHUNK, CHUNK)
    to_pair = lambda t: jnp.repeat(t.reshape(N_PAIRS, 2, CHUNK).transpose(0, 2, 1), HEAD_DIM, axis=2)
    cd = jnp.repeat(chunk_decay.reshape(N_PAIRS, 1, 2), HEAD_DIM, axis=2)
    return intra_st, to_pair(q_decay), to_pair(k_decay), cd


def _retention(z, cos, sin, gain, tm):
    s = z.shape[0]
    intra_st, qd, kd, cd = _retention_tables()
    zspec = lambda c: pl.BlockSpec((tm, MIX_W), lambda i: (i, c))
    full = lambda a: pl.BlockSpec(a.shape, lambda i: (0,) * a.ndim)
    return pl.pallas_call(
        _retention_body,
        grid=(s // tm,),
        in_specs=[zspec(C_TQ), zspec(C_TK), zspec(C_TV), zspec(C_TG),
                  pl.BlockSpec((tm, PAIR_W), lambda i: (i, 0)),
                  pl.BlockSpec((tm, PAIR_W), lambda i: (i, 0)),
                  full(intra_st), full(qd), full(kd), full(cd),
                  pl.BlockSpec((1, MIX_W), lambda i: (0, 0))],
        out_specs=pl.BlockSpec((tm, MIX_W), lambda i: (i, 0)),
        out_shape=jax.ShapeDtypeStruct((s, MIX_W), BF16),
        scratch_shapes=[pltpu.VMEM((N_PAIRS, PAIR_W, PAIR_W), F32),
                        pltpu.VMEM((tm, MIX_W), F32), pltpu.VMEM((tm, MIX_W), F32),
                        pltpu.VMEM((tm, MIX_W), F32)],
        compiler_params=_cparams(("arbitrary",)),
        name="retention",
    )(z, z, z, z, cos, sin, intra_st, qd, kd, cd, gain)


def _load_with_halo(ext_ref, cur_ref, halo_ref, first):
    halo = halo_ref[...]
    ext_ref[0:HALO, :] = jnp.where(first, jnp.zeros_like(halo), halo)
    ext_ref[HALO:, :] = cur_ref[...]


def _mlstm_body(zq_ref, zk_ref, hq_ref, hk_ref, zv_ref, zo_ref, g_ref, mu_ref, u_ref, mprev_ref,
                mulast_ref, urow_ref, spread_ref, cw_ref, cb_ref, gain_ref, o_ref, c_ref, n_ref,
                ext_scr, q_scr, k_scr, y_scr, mu_scr, inter_scr, floor_scr, w_scr, dec_scr):
    tm = zq_ref.shape[0]
    first = pl.program_id(0) == 0

    @pl.when(first)
    def _():
        c_ref[...] = jnp.zeros_like(c_ref)
        n_ref[...] = jnp.zeros_like(n_ref)

    lane_g = lax.broadcasted_iota(jnp.int32, (1, 128), 1)
    ml_lane = (lane_g >= G_ML_F) & (lane_g < G_ML_F + N_HEADS)
    spread = spread_ref[...]
    to_heads = lambda t: _dot_exact_rhs(jnp.where(ml_lane, t, 0.0), spread)
    mu = mu_ref[...]
    m_prev = mprev_ref[...]
    mu_last = mulast_ref[...]
    mu_scr[...] = to_heads(mu)
    inter_scr[...] = to_heads(jnp.exp(m_prev - mu))
    floor_scr[...] = to_heads(jnp.exp(-(g_ref[...] + mu)))
    w_scr[...] = to_heads(jnp.exp(u_ref[...] - mu_last))
    dec_scr[...] = to_heads(jnp.exp(m_prev - mu_last))

    for part, (cur, halo, dst, scale) in enumerate(((zq_ref, hq_ref, q_scr, 1.0),
                                                    (zk_ref, hk_ref, k_scr, HEAD_DIM ** -0.5))):
        _load_with_halo(ext_scr, cur, halo, first)
        cols = slice(part * MIX_W, (part + 1) * MIX_W)
        acc = cb_ref[:, cols]
        for tap in range(CONV_W):
            off = HALO - (CONV_W - 1) + tap
            acc = acc + cw_ref[tap:tap + 1, cols] * ext_scr[off:off + tm, :]
        dst[...] = _silu(acc) * scale

    lane = lax.broadcasted_iota(jnp.int32, (1, PAIR_W), 1)
    m0 = lane < HEAD_DIM
    bd, _, _ = _block_diag_mask()
    bd_ones = jnp.where(bd, 1.0, 0.0).astype(BF16)
    causal = (lax.broadcasted_iota(jnp.int32, (CHUNK, PAIR_W), 1) % HEAD_DIM
              <= lax.broadcasted_iota(jnp.int32, (CHUNK, PAIR_W), 0))

    pairs = range(N_PAIRS)
    pair_cols = [slice(p * PAIR_W, (p + 1) * PAIR_W) for p in pairs]

    def chunk(c, carry):
        rows = pl.ds(pl.multiple_of(c * CHUNK, CHUNK), CHUNK)
        first8 = pl.ds(pl.multiple_of(c * CHUNK, CHUNK), 8)
        u_rows = urow_ref[c]
        q = [q_scr[rows, cols] for cols in pair_cols]
        k = [k_scr[rows, cols] for cols in pair_cols]
        v = [zv_ref[rows, cols] for cols in pair_cols]
        cst = [c_ref[p] for p in pairs]
        nst = [n_ref[p] for p in pairs]
        qb = [q[p].astype(BF16) for p in pairs]
        sc = [_dot_nt(qb[p], _stack_heads(k[p], m0).astype(BF16)) for p in pairs]
        q_c = [_dot(qb[p], cst[p].astype(BF16)) for p in pairs]
        q_n = [_seg_sum(q[p] * nst[p], bd_ones) for p in pairs]
        d_mat = [jnp.exp(u_rows[p:p + 1, :] - mu_scr[rows, pair_cols[p]]) for p in pairs]
        pr = [jnp.where(causal, sc[p] * d_mat[p], 0.0).astype(BF16) for p in pairs]
        v_ones = [jnp.concatenate([_stack_heads(v[p], m0).astype(BF16), bd_ones], axis=1) for p in pairs]
        nd = [_dot(pr[p], v_ones[p]) for p in pairs]
        kw = [k[p] * w_scr[rows, pair_cols[p]] for p in pairs]
        upd = [_dot_tn(kw[p].astype(BF16), v[p].astype(BF16)) for p in pairs]
        for p in pairs:
            inter = inter_scr[rows, pair_cols[p]]
            num = nd[p][:, :PAIR_W] + inter * q_c[p]
            den = nd[p][:, PAIR_W:] + inter * q_n[p]
            y_scr[rows, pair_cols[p]] = num / jnp.maximum(jnp.abs(den), floor_scr[rows, pair_cols[p]])
            dec = dec_scr[first8, pair_cols[p]][0:1]
            c_ref[p] = dec * cst[p] + jnp.where(bd, upd[p], 0.0)
            n_ref[p] = dec * nst[p] + jnp.sum(kw[p], axis=0, keepdims=True)
        return carry

    lax.fori_loop(0, tm // CHUNK, chunk, 0)
    y = y_scr[...] * _sigmoid(zo_ref[...])
    ms = _seg_sum(y * y, _seg_matrix(MIX_W)) * (1.0 / HEAD_DIM)
    o_ref[...] = (y * lax.rsqrt(ms + NORM_EPS) * gain_ref[...]).astype(o_ref.dtype)


def _halo_spec(tm, width, c):
    step = tm // HALO
    return pl.BlockSpec((HALO, width), lambda i: (jnp.maximum(i * step - 1, 0), c))


def _mlstm(z, gate_cols, urow, conv_w, conv_b, gain, tm):
    s = z.shape[0]
    spread = np.zeros((128, MIX_W), np.float32)
    for h in range(N_HEADS):
        spread[G_ML_F + h, h * HEAD_DIM:(h + 1) * HEAD_DIM] = 1.0
    zspec = lambda c: pl.BlockSpec((tm, MIX_W), lambda i: (i, c))
    col = pl.BlockSpec((tm, 128), lambda i: (i, 0))
    big = pltpu.VMEM((tm, MIX_W), F32)
    return pl.pallas_call(
        _mlstm_body,
        grid=(s // tm,),
        in_specs=[zspec(C_MQ), zspec(C_MK), _halo_spec(tm, MIX_W, C_MQ), _halo_spec(tm, MIX_W, C_MK),
                  zspec(C_MV), zspec(C_MO),
                  col, col, col, col, col,
                  pl.BlockSpec((tm // CHUNK, N_PAIRS, PAIR_W), lambda i: (i, 0, 0)),
                  pl.BlockSpec((128, MIX_W), lambda i: (0, 0)),
                  pl.BlockSpec((CONV_W, 2 * MIX_W), lambda i: (0, 0)),
                  pl.BlockSpec((1, 2 * MIX_W), lambda i: (0, 0)),
                  pl.BlockSpec((1, MIX_W), lambda i: (0, 0))],
        out_specs=pl.BlockSpec((tm, MIX_W), lambda i: (i, 0)),
        out_shape=jax.ShapeDtypeStruct((s, MIX_W), BF16),
        scratch_shapes=[pltpu.VMEM((N_PAIRS, PAIR_W, PAIR_W), F32),
                        pltpu.VMEM((N_PAIRS, 1, PAIR_W), F32),
                        pltpu.VMEM((tm + HALO, MIX_W), F32),
                        big, big, big, big, big, big, big, big],
        compiler_params=_cparams(("arbitrary",)),
        name="mlstm",
    )(z, z, z, z, z, z, *gate_cols, urow, jnp.asarray(spread, BF16), conv_w, conv_b, gain)


def _rwkv_body(zr_ref, zk_ref, zv_ref, zl_ref, hr_ref, hk_ref, hv_ref, hl_ref, mu_ref, mul_ref,
               wl_ref, w0_ref, a0_ref, kk_ref, ka_ref, rk_ref, gnw_ref, gnb_ref, o_ref,
               h_ref, ext_scr, extl_scr, rt_scr, kt_scr, kb_scr, bb_scr, kh_scr, bh_scr,
               gam_scr, v_scr, y_scr):
    tm = zr_ref.shape[0]
    first = pl.program_id(0) == 0

    @pl.when(first)
    def _():
        h_ref[...] = jnp.zeros_like(h_ref)

    def shifted(ext, cur_ref, halo_ref, mu):
        _load_with_halo(ext, cur_ref, halo_ref, first)
        cur = ext[HALO:HALO + tm, :]
        prev = ext[HALO - 1:HALO - 1 + tm, :]
        return cur + mu * (prev - cur)

    r = shifted(ext_scr, zr_ref, hr_ref, mu_ref[:, 0:MIX_W])
    k = shifted(ext_scr, zk_ref, hk_ref, mu_ref[:, MIX_W:2 * MIX_W])
    v = shifted(ext_scr, zv_ref, hv_ref, mu_ref[:, 2 * MIX_W:3 * MIX_W])
    lora_in = shifted(extl_scr, zl_ref, hl_ref, mul_ref[...])
    lane_l = lax.broadcasted_iota(jnp.int32, (1, LORA_W), 1)
    act = jnp.where(lane_l < 64, jnp.tanh(lora_in), jnp.where(lane_l < 128, lora_in, _sigmoid(lora_in)))
    lora = _dot(act.astype(BF16), wl_ref[...])
    logw = -_sigmoid(w0_ref[...] + lora[:, 0:MIX_W]) * math.exp(-0.5)
    a = _sigmoid(a0_ref[...] + lora[:, MIX_W:2 * MIX_W])
    gate = lora[:, 2 * MIX_W:3 * MIX_W]

    seg = _seg_matrix(MIX_W)
    kk = k * kk_ref[...]
    kk = kk * lax.rsqrt(jnp.maximum(_seg_sum(kk * kk, seg), 1e-12))
    k2 = k * (1.0 + (a - 1.0) * ka_ref[...])
    b = kk * a
    bonus = _seg_sum(r * k2 * rk_ref[...], seg) * v

    ri = lax.broadcasted_iota(jnp.int32, (tm, tm), 0)
    ci = lax.broadcasted_iota(jnp.int32, (tm, tm), 1)
    same = (ri // CHUNK) == (ci // CHUNK)
    lower = jnp.where(same & (ci <= ri), 1.0, 0.0).astype(BF16)
    upper = jnp.where(same & (ci > ri), 1.0, 0.0).astype(BF16)
    cl = _dot_exact_lhs(lower, logw)
    cs = _dot_exact_lhs(upper, logw)
    e_cl = jnp.exp(cl)
    e_ncl = jnp.exp(-cl)
    e_cs = jnp.exp(cs)
    rt_scr[...] = r * e_cl
    kt_scr[...] = kk * jnp.exp(cl - logw)
    kb_scr[...] = k2 * e_ncl
    bb_scr[...] = b * e_ncl
    kh_scr[...] = k2 * e_cs
    bh_scr[...] = b * e_cs
    gam_scr[...] = jnp.exp(cl + cs)
    v_scr[...] = v

    lane = lax.broadcasted_iota(jnp.int32, (1, PAIR_W), 1)
    m0 = lane < HEAD_DIM
    bd, rr, cc = _block_diag_mask()
    strict = bd & (cc < rr)
    incl = bd & (cc <= rr)
    eye = bd & (cc == rr)
    eye_f = jnp.where(eye, 1.0, 0.0)

    pairs = range(N_PAIRS)
    pair_cols = [slice(p * PAIR_W, (p + 1) * PAIR_W) for p in pairs]

    chunks_per_trip = min(4, tm // CHUNK)
    units = [(cc, p) for cc in range(chunks_per_trip) for p in pairs]

    def trip(c, carry):
        base = pl.multiple_of(c * (chunks_per_trip * CHUNK), chunks_per_trip * CHUNK)
        rows = [pl.ds(base + cc * CHUNK, CHUNK) for cc in range(chunks_per_trip)]
        first8 = [pl.ds(base + cc * CHUNK, 8) for cc in range(chunks_per_trip)]
        stacked = lambda scr: [_stack_heads(scr[rows[cc], pair_cols[p]], m0) for cc, p in units]
        rt_st = stacked(rt_scr)
        kt_st = stacked(kt_scr)
        v_st = [t.astype(BF16) for t in stacked(v_scr)]
        kh_st = [t.astype(BF16) for t in stacked(kh_scr)]
        bh_st = [t.astype(BF16) for t in stacked(bh_scr)]
        twice = lambda scr: [jnp.concatenate([scr[rows[cc], pair_cols[p]].astype(BF16)] * 2, axis=0)
                             for cc, p in units]
        bb2 = twice(bb_scr)
        kb2 = twice(kb_scr)
        gam = [gam_scr[first8[cc], pair_cols[p]][0:1] for cc, p in units]
        us = range(len(units))
        lhs = [jnp.concatenate([kt_st[u], rt_st[u]], axis=0).astype(BF16) for u in us]
        g_b = [_dot_nt(lhs[u], bb2[u]) for u in us]
        g_k = [_dot_nt(lhs[u], kb2[u]) for u in us]
        x = [jnp.where(strict, -g_b[u][:PAIR_W], 0.0) for u in us]
        q_bd = [jnp.where(incl, g_b[u][PAIR_W:], 0.0).astype(BF16) for u in us]
        b_bd = [jnp.where(strict, g_k[u][:PAIR_W], 0.0).astype(BF16) for u in us]
        p_bd = [jnp.where(incl, g_k[u][PAIR_W:], 0.0).astype(BF16) for u in us]
        bv = [_dot(b_bd[u], v_st[u]).astype(BF16) for u in us]
        pv = [_dot(p_bd[u], v_st[u]) for u in us]
        khv = [_dot_tn(kh_st[u], v_st[u]) for u in us]
        t_inv = [eye_f + x[u] for u in us]
        for _ in range(5):
            xb = [x[u].astype(BF16) for u in us]
            x = [_dot(xb[u], xb[u]) for u in us]
            t_inv = [t_inv[u] + _dot(t_inv[u].astype(BF16), x[u].astype(BF16)) for u in us]
        tb = [t_inv[u].astype(BF16) for u in us]
        kt2 = [_dot(tb[u], kt_st[u].astype(BF16)).astype(BF16) for u in us]
        w1 = [_dot(tb[u], bv[u]).astype(BF16) for u in us]
        y1 = [(rt_st[u] - _dot(q_bd[u], kt2[u])).astype(BF16) for u in us]
        y0 = [pv[u] - _dot(q_bd[u], w1[u]) for u in us]
        m_mat = [(jnp.where(eye, gam[u], 0.0) - _dot_tn(bh_st[u], kt2[u])).astype(BF16) for u in us]
        n_mat = [khv[u] - _dot_tn(bh_st[u], w1[u]) for u in us]
        h = [h_ref[p] for p in pairs]
        for u, (cc, p) in enumerate(units):
            hb = h[p].astype(BF16)
            y_st = _dot(y1[u], hb) + y0[u]
            y_scr[rows[cc], pair_cols[p]] = y_st[:CHUNK] + y_st[CHUNK:]
            h[p] = _dot(m_mat[u], hb) + n_mat[u]
        for p in pairs:
            h_ref[p] = h[p]
        return carry

    lax.fori_loop(0, tm // (chunks_per_trip * CHUNK), trip, 0)

    y = y_scr[...]
    mean = _seg_sum(y, seg) * (1.0 / HEAD_DIM)
    yc = y - mean
    var = _seg_sum(yc * yc, seg) * (1.0 / HEAD_DIM)
    yn = yc * lax.rsqrt(var + RWKV_GN_EPS) * gnw_ref[...] + gnb_ref[...]
    o_ref[...] = ((yn + bonus) * gate).astype(o_ref.dtype)


def _rwkv(z, mu, mu_l, w_lora, w0, a0, k_k, k_a, r_k, gn_w, gn_b, tm):
    s = z.shape[0]
    zspec = lambda c: pl.BlockSpec((tm, MIX_W), lambda i: (i, c))
    row = lambda w: pl.BlockSpec((1, w), lambda i: (0, 0))
    big = pltpu.VMEM((tm, MIX_W), F32)
    return pl.pallas_call(
        _rwkv_body,
        grid=(s // tm,),
        in_specs=[zspec(C_RR), zspec(C_RK), zspec(C_RV),
                  pl.BlockSpec((tm, LORA_W), lambda i: (i, C_RL_256)),
                  _halo_spec(tm, MIX_W, C_RR), _halo_spec(tm, MIX_W, C_RK), _halo_spec(tm, MIX_W, C_RV),
                  _halo_spec(tm, LORA_W, C_RL_256),
                  row(3 * MIX_W), row(LORA_W),
                  pl.BlockSpec((LORA_W, 3 * MIX_W), lambda i: (0, 0)),
                  row(MIX_W), row(MIX_W), row(MIX_W), row(MIX_W), row(MIX_W), row(MIX_W), row(MIX_W)],
        out_specs=pl.BlockSpec((tm, MIX_W), lambda i: (i, 0)),
        out_shape=jax.ShapeDtypeStruct((s, MIX_W), BF16),
        scratch_shapes=[pltpu.VMEM((N_PAIRS, PAIR_W, PAIR_W), F32),
                        pltpu.VMEM((tm + HALO, MIX_W), F32), pltpu.VMEM((tm + HALO, LORA_W), F32),
                        big, big, big, big, big, big, big, big, big],
        compiler_params=_cparams(("arbitrary",)),
        name="rwkv7",
    )(z, z, z, z, z, z, z, z, mu, mu_l, w_lora, w0, a0, k_k, k_a, r_k, gn_w, gn_b)


def _w_in_column_map():
    fox, rwkv, ret, ml = 0, 2056, 3848, 5896
    src = np.full((Z_COLS,), -1, np.int64)

    def put(dst, start, n):
        src[dst:dst + n] = np.arange(start, start + n)

    for blk, start in ((C_FQ, fox), (C_FK, fox + 512), (C_FV, fox + 1024), (C_FO, fox + 1536),
                       (C_RR, rwkv), (C_RK, rwkv + 512), (C_RV, rwkv + 1024),
                       (C_TQ, ret), (C_TK, ret + 512), (C_TV, ret + 1024), (C_TG, ret + 1536),
                       (C_MQ, ml), (C_MK, ml + 512), (C_MV, ml + 1024), (C_MO, ml + 1536)):
        put(blk * MIX_W, start, MIX_W)
    put(C_RL_256 * LORA_W, rwkv + 1536, LORA_W)
    gate0 = C_GATE_128 * 128
    put(gate0 + G_FOX_F, fox + 2048, N_HEADS)
    put(gate0 + G_ML_I, ml + 2048, N_HEADS)
    put(gate0 + G_ML_F, ml + 2056, N_HEADS)
    return src


def _w_in_runs():
    src = _w_in_column_map()
    runs, start = [], 0
    for pos in range(1, Z_COLS + 1):
        run_continues = pos < Z_COLS and (
            (src[pos] < 0 and src[pos - 1] < 0) or (src[pos - 1] >= 0 and src[pos] == src[pos - 1] + 1))
        if not run_continues:
            runs.append((start, int(src[start]), pos - start))
            start = pos
    return runs


def _w_in_body(w_ref, o_ref):
    for dst, src, n in _w_in_runs():
        if src < 0:
            o_ref[:, dst:dst + n] = jnp.zeros((o_ref.shape[0], n), BF16)
        else:
            o_ref[:, dst:dst + n] = w_ref[:, src:src + n].astype(BF16)


def _permute_w_in(w_in, layer, tr):
    _, d, d_in = w_in.shape
    return pl.pallas_call(
        _w_in_body,
        grid=(d // tr,),
        in_specs=[pl.BlockSpec((None, tr, d_in), lambda i: (layer, i, 0))],
        out_specs=pl.BlockSpec((tr, Z_COLS), lambda i: (i, 0)),
        out_shape=jax.ShapeDtypeStruct((d, Z_COLS), BF16),
        compiler_params=_cparams(("parallel",)),
        name="w_in_layout",
    )(w_in)


def _merge_gate_body(w_ref, o_ref):
    d = w_ref.shape[2]
    for n in range(w_ref.shape[1]):
        o_ref[:, n * d:(n + 1) * d] = w_ref[:, n, :].astype(BF16)


def _merge_gate_bf16(w, layer, tr):
    _, r, nb, c = w.shape
    return pl.pallas_call(
        _merge_gate_body,
        grid=(r // tr,),
        in_specs=[pl.BlockSpec((None, tr, nb, c), lambda i: (layer, i, 0, 0))],
        out_specs=pl.BlockSpec((tr, nb * c), lambda i: (i, 0)),
        out_shape=jax.ShapeDtypeStruct((r, nb * c), BF16),
        compiler_params=_cparams(("parallel",)),
        name="merge_gate_bf16",
    )(w)


def _branch_bf16(w, layer):
    _, nb, r, c = w.shape
    return pl.pallas_call(
        _cast_body,
        grid=(nb,),
        in_specs=[pl.BlockSpec((None, None, r, c), lambda n: (layer, n, 0, 0))],
        out_specs=pl.BlockSpec((None, r, c), lambda n: (n, 0, 0)),
        out_shape=jax.ShapeDtypeStruct((nb, r, c), BF16),
        compiler_params=_cparams(("parallel",)),
        name="branch_bf16",
    )(w)


def _cast_body(w_ref, o_ref):
    o_ref[...] = w_ref[...].astype(BF16)


def _layer_bf16(w, layer):
    _, r, c = w.shape
    tr = r
    while tr * c > CAST_BLOCK_ELEMS and tr % 16 == 0:
        tr //= 2
    return pl.pallas_call(
        _cast_body,
        grid=(r // tr,),
        in_specs=[pl.BlockSpec((None, tr, c), lambda i: (layer, i, 0))],
        out_specs=pl.BlockSpec((tr, c), lambda i: (i, 0)),
        out_shape=jax.ShapeDtypeStruct((r, c), BF16),
        compiler_params=_cparams(("parallel",)),
        name="weight_bf16",
    )(w)


def kernel(x, positions, ffn1_norm, ffn1_w1, ffn1_w3, ffn1_w2, mix_norm, w_in, fox_f_bias, fox_q_gain, fox_k_gain, rwkv_shift_mu, rwkv_w0, rwkv_w2, rwkv_a0, rwkv_a2, rwkv_g2, rwkv_k_k, rwkv_k_a, rwkv_r_k, rwkv_gn_w, rwkv_gn_b, ret_gn_gain, mlstm_conv_w, mlstm_conv_b, mlstm_i_bias, mlstm_f_bias, mlstm_gn_gain, w_merge_gate, merge_gate_bias, w_branch, w_out, ffn2_norm, ffn2_w1, ffn2_w3, ffn2_w2):
    batch, seq, d = x.shape
    depth = w_in.shape[0]
    dff = ffn1_w1.shape[-1]
    pick = lambda full, want: want if full % want == 0 else full
    tm_dense = pick(seq, 512)
    tm_ffn = pick(seq, 1024)
    tm_inproj = pick(seq, 1024)
    tf = pick(dff, 512)
    te = pick(d, 512)
    tn = pick(Z_COLS, 1024)
    tm_row = pick(seq, 512)
    tm_seq = pick(seq, 256)
    tq = pick(seq, 512)
    row = lambda t: t.reshape(1, -1).astype(F32)

    outs = []
    for bi in range(batch):
        xb = x[bi]
        pos_col = positions[bi].astype(F32).reshape(seq, 1)
        cos, sin = _rope_tables(pos_col, tm_row)
        for l in range(depth):
            xb = _ffn(xb, row(ffn1_norm[l]), _layer_bf16(ffn1_w1, l), _layer_bf16(ffn1_w3, l),
                      _layer_bf16(ffn1_w2, l), tm_ffn, tf)
            z = _inproj(xb, row(mix_norm[l]), _permute_w_in(w_in, l, pick(d, 256)), tm_inproj, tn)

            gate_bias = jnp.zeros((128,), F32)
            gate_bias = gate_bias.at[G_FOX_F:G_FOX_F + N_HEADS].set(fox_f_bias[l])
            gate_bias = gate_bias.at[G_ML_I:G_ML_I + N_HEADS].set(mlstm_i_bias[l])
            gate_bias = gate_bias.at[G_ML_F:G_ML_F + N_HEADS].set(mlstm_f_bias[l])
            g, ml_mu, ml_u, ml_mprev, ml_mulast, ml_urow = _gates(z, gate_bias[None, :], tm_row)

            bound = (LOG2E * HEAD_DIM ** 0.5) * jnp.max(jnp.abs(fox_q_gain[l])) * jnp.max(jnp.abs(fox_k_gain[l]))
            bounded = bound <= FOX_MAX_STATIC_BOUND
            shift = jnp.where(bounded, bound, 0.0) * jnp.ones((1, 128), F32)
            fq, fk, fv = _fox_prep(z, g, shift, row(jnp.tile(fox_q_gain[l], N_HEADS)),
                                   row(jnp.tile(fox_k_gain[l], N_HEADS)), tm_row)
            y_fox = lax.cond(bounded,
                             lambda *a: _fox_attention(*a, tq, True),
                             lambda *a: _fox_attention(*a, tq, False), fq, fk, fv, z)

            mu = rwkv_shift_mu[l]
            w_lora = jnp.zeros((LORA_W, 3 * MIX_W), F32)
            w_lora = w_lora.at[0:64, 0:MIX_W].set(rwkv_w2[l])
            w_lora = w_lora.at[64:128, MIX_W:2 * MIX_W].set(rwkv_a2[l])
            w_lora = w_lora.at[128:256, 2 * MIX_W:3 * MIX_W].set(rwkv_g2[l])
            y_rwkv = _rwkv(z, row(mu[:3 * MIX_W]), row(mu[3 * MIX_W:]), w_lora.astype(BF16),
                           row(rwkv_w0[l]), row(rwkv_a0[l]), row(rwkv_k_k[l]), row(rwkv_k_a[l]),
                           row(rwkv_r_k[l]), row(rwkv_gn_w[l]), row(rwkv_gn_b[l]), tm_seq)

            y_ret = _retention(z, cos, sin, row(ret_gn_gain[l]), tm_seq)

            y_m = _mlstm(z, (g, ml_mu, ml_u, ml_mprev, ml_mulast), ml_urow,
                         mlstm_conv_w[l].astype(F32), row(mlstm_conv_b[l]), row(mlstm_gn_gain[l]), tm_seq)

            xb = _merge(xb, row(mix_norm[l]), (y_fox, y_rwkv, y_ret, y_m),
                        _merge_gate_bf16(w_merge_gate, l, pick(d, 64)), merge_gate_bias[l].astype(F32),
                        _branch_bf16(w_branch, l), _layer_bf16(w_out, l), tm_dense, te)
            xb = _ffn(xb, row(ffn2_norm[l]), _layer_bf16(ffn2_w1, l), _layer_bf16(ffn2_w3, l),
                      _layer_bf16(ffn2_w2, l), tm_ffn, tf)
        outs.append(xb)
    return jnp.stack(outs, axis=0)
```
